```python
import math
import jax, jax.numpy as jnp
from jax import lax
import numpy as np

D_MODEL = 1024
BATCH = 32
SEQ = 2048
DEPTH = 4

CHUNK = 64
N_MIXERS = 2
N_SB_LAYERS = (DEPTH + 1) // 2
N_CV_LAYERS = DEPTH // 2
SB_HEADS = 16
SB_HEAD_DIM = D_MODEL // SB_HEADS
Q_BLOCK = 128
CONV_WIDTH = 31
N_EXPERTS = 16
N_GROUPS = 4
EXPERTS_PER_GROUP = N_EXPERTS // N_GROUPS
TOP_K = 2
D_EXPERT = 512
EXPERT_BLOCK = 256
NORM_EPS = 1e-6

kernel_name = "hybrid_stickbreak_conformer_groupmoe"


def rms_norm(x, g):
    xf = x.astype(jnp.float32)
    y = xf * lax.rsqrt(jnp.mean(xf * xf, axis=-1, keepdims=True) + NORM_EPS)
    return (y * g.astype(jnp.float32)).astype(x.dtype)


def layer_norm(x, g, b):
    xf = x.astype(jnp.float32)
    mu = jnp.mean(xf, axis=-1, keepdims=True)
    var = jnp.mean(jnp.square(xf - mu), axis=-1, keepdims=True)
    y = (xf - mu) * lax.rsqrt(var + NORM_EPS) * g.astype(jnp.float32) + b.astype(jnp.float32)
    return y.astype(x.dtype)


def modulate(h, shift, scale):
    return h * (1 + scale[:, None, :]) + shift[:, None, :]


def stick_breaking_attention(q, k, v):
    S = q.shape[2]
    inv_sqrt_d = 1.0 / math.sqrt(q.shape[-1])
    outs = []
    for qb in range(S // Q_BLOCK):
        q0, q1 = qb * Q_BLOCK, (qb + 1) * Q_BLOCK
        qi = q[:, :, q0:q1]
        kj = k[:, :, :q1]
        vj = v[:, :, :q1]
        z = jnp.einsum('bhqd,bhkd->bhqk', qi, kj).astype(jnp.float32) * inv_sqrt_d
        t_idx = q0 + jnp.arange(Q_BLOCK)[:, None]
        s_idx = jnp.arange(q1)[None, :]
        mask = s_idx < t_idx
        log_beta = jax.nn.log_sigmoid(z)
        log_1m_beta = jnp.where(mask, jax.nn.log_sigmoid(-z), 0.0)
        acc = lax.cumsum(log_1m_beta, axis=3, reverse=True) - log_1m_beta
        a = jnp.where(mask, jnp.exp(log_beta + acc), 0.0)
        outs.append(jnp.einsum('bhqk,bhkd->bhqd', a.astype(v.dtype), vj))
    return jnp.concatenate(outs, axis=2)


def stick_breaking_mixer(h, w_qkv, w_o):
    B, S, D = h.shape
    qkv = (h @ w_qkv).reshape(B, S, 3, SB_HEADS, SB_HEAD_DIM)
    q = qkv[:, :, 0].transpose(0, 2, 1, 3)
    k = qkv[:, :, 1].transpose(0, 2, 1, 3)
    v = qkv[:, :, 2].transpose(0, 2, 1, 3)
    o = stick_breaking_attention(q, k, v).transpose(0, 2, 1, 3).reshape(B, S, D)
    return o @ w_o


def conformer_conv_module(h, w_in, b_in, w_dw, b_dw, ln_g, ln_b, w_out, b_out):
    D = h.shape[-1]
    u = h @ w_in + b_in
    u = u[..., :D] * jax.nn.sigmoid(u[..., D:])
    u = lax.conv_general_dilated(
        u, w_dw[:, None, :].astype(u.dtype),
        window_strides=(1,), padding=[(CONV_WIDTH - 1, 0)],
        dimension_numbers=('NWC', 'WIO', 'NWC'), feature_group_count=D) + b_dw
    u = jax.nn.silu(layer_norm(u, ln_g, ln_b))
    return u @ w_out + b_out


def route(hf, w_router, b_router):
    T = hf.shape[0]
    scores = jax.nn.sigmoid((hf @ w_router).astype(jnp.float32))
    sel = scores + b_router.astype(jnp.float32)
    sel_g = sel.reshape(T, N_GROUPS, EXPERTS_PER_GROUP)
    group_score = jnp.sum(lax.top_k(sel_g, TOP_K)[0], axis=-1)
    g = jnp.argmax(group_score, axis=-1)
    in_group = jnp.take_along_axis(sel_g, g[:, None, None], axis=1)[:, 0]
    _, local = lax.top_k(in_group, TOP_K)
    idx = g[:, None] * EXPERTS_PER_GROUP + local
    w = jnp.take_along_axis(scores, idx, axis=1)
    w = w / jnp.sum(w, axis=-1, keepdims=True)
    return idx.astype(jnp.int32), w


def moe_ffn(h, w_router, b_router, w_gate, w_up, w_down):
    B, S, D = h.shape
    T = B * S
    TK = T * TOP_K
    hf = h.reshape(T, D)
    idx, wts = route(hf, w_router, b_router)
    flat_e = idx.reshape(-1)
    flat_tok = jnp.repeat(jnp.arange(T, dtype=jnp.int32), TOP_K)
    flat_w = wts.reshape(-1)
    counts = jax.ops.segment_sum(jnp.ones((TK,), jnp.int32), flat_e, num_segments=N_EXPERTS)
    padded = (counts + EXPERT_BLOCK - 1) // EXPERT_BLOCK * EXPERT_BLOCK
    pad_end = jnp.cumsum(padded)
    pad_start = pad_end - padded
    start = jnp.cumsum(counts) - counts
    order = jnp.argsort(flat_e)
    se = flat_e[order]
    dest = pad_start[se] + (jnp.arange(TK, dtype=jnp.int32) - start[se])
    n_rows = TK + N_EXPERTS * EXPERT_BLOCK
    n_blk = n_rows // EXPERT_BLOCK
    row_tok = jnp.full((n_rows,), T, jnp.int32).at[dest].set(flat_tok[order])
    row_w = jnp.zeros((n_rows,), jnp.float32).at[dest].set(flat_w[order])
    blk_e = jnp.minimum(jnp.searchsorted(pad_end, jnp.arange(n_blk) * EXPERT_BLOCK, side='right'),
                        N_EXPERTS - 1).astype(jnp.int32)
    h_pad = jnp.concatenate([hf, jnp.zeros((1, D), hf.dtype)], axis=0)
    xb = h_pad[row_tok].reshape(n_blk, EXPERT_BLOCK, D)

    def expert_block(args):
        xe, e = args
        return (jax.nn.silu(xe @ w_gate[e]) * (xe @ w_up[e])) @ w_down[e]

    yb = lax.map(expert_block, (xb, blk_e)).reshape(n_rows, D)
    y = jnp.zeros((T + 1, D), yb.dtype).at[row_tok].add(yb * row_w[:, None].astype(yb.dtype))
    return y[:T].reshape(B, S, D)


def setup_inputs(seed: int = 0) -> dict:
    key = jax.random.key(seed)
    ks = jax.random.split(key, 24)
    D, F, E = D_MODEL, D_EXPERT, N_EXPERTS
    nrm = lambda k, shape, s: jax.random.normal(k, shape, jnp.float32) * s
    return {
        "x": nrm(ks[0], (BATCH, SEQ, D), 1.0),
        "c": nrm(ks[1], (BATCH, D), 1.0),
        "norm1_g": 1.0 + nrm(ks[2], (DEPTH, D), 0.02),
        "norm2_g": 1.0 + nrm(ks[3], (DEPTH, D), 0.02),
        "w_ada": nrm(ks[4], (DEPTH, D, 6 * D), 0.5 * D ** -0.5),
        "b_ada": nrm(ks[5], (DEPTH, 6 * D), 0.02),
        "sb_w_qkv": nrm(ks[6], (N_SB_LAYERS, D, 3 * D), D ** -0.5),
        "sb_w_o": nrm(ks[7], (N_SB_LAYERS, D, D), D ** -0.5),
        "cv_w_in": nrm(ks[8], (N_CV_LAYERS, D, 2 * D), D ** -0.5),
        "cv_b_in": nrm(ks[9], (N_CV_LAYERS, 2 * D), 0.02),
        "cv_w_dw": nrm(ks[10], (N_CV_LAYERS, CONV_WIDTH, D), CONV_WIDTH ** -0.5),
        "cv_b_dw": nrm(ks[11], (N_CV_LAYERS, D), 0.02),
        "cv_ln_g": 1.0 + nrm(ks[12], (N_CV_LAYERS, D), 0.02),
        "cv_ln_b": nrm(ks[13], (N_CV_LAYERS, D), 0.02),
        "cv_w_out": nrm(ks[14], (N_CV_LAYERS, D, D), D ** -0.5),
        "cv_b_out": nrm(ks[15], (N_CV_LAYERS, D), 0.02),
        "w_router": nrm(ks[16], (D, E), D ** -0.5),
        "b_router": nrm(ks[17], (E,), 0.01),
        "moe_w_gate": nrm(ks[18], (DEPTH, E, D, F), D ** -0.5),
        "moe_w_up": nrm(ks[19], (DEPTH, E, D, F), D ** -0.5),
        "moe_w_down": nrm(ks[20], (DEPTH, E, F, D), F ** -0.5),
        "final_g": 1.0 + nrm(ks[21], (D,), 0.02),
    }


def reference(x, c, norm1_g, norm2_g, w_ada, b_ada, sb_w_qkv, sb_w_o,
              cv_w_in, cv_b_in, cv_w_dw, cv_b_dw, cv_ln_g, cv_ln_b, cv_w_out, cv_b_out,
              w_router, b_router, moe_w_gate, moe_w_up, moe_w_down, final_g):
    cond = jax.nn.silu(c)
    for layer in range(DEPTH):
        mod = cond @ w_ada[layer] + b_ada[layer]
        sh1, sc1, g1, sh2, sc2, g2 = jnp.split(mod, 6, axis=-1)
        h = modulate(rms_norm(x, norm1_g[layer]), sh1, sc1)
        j = layer // N_MIXERS
        if layer % N_MIXERS == 0:
            y = stick_breaking_mixer(h, sb_w_qkv[j], sb_w_o[j])
        else:
            y = conformer_conv_module(h, cv_w_in[j], cv_b_in[j], cv_w_dw[j], cv_b_dw[j],
                                      cv_ln_g[j], cv_ln_b[j], cv_w_out[j], cv_b_out[j])
        x = x + g1[:, None, :] * y
        h = modulate(rms_norm(x, norm2_g[layer]), sh2, sc2)
        x = x + g2[:, None, :] * moe_ffn(h, w_router, b_router,
                                        moe_w_gate[layer], moe_w_up[layer], moe_w_down[layer])
    return rms_norm(x, final_g)
```

```python
import functools

import jax
import jax.numpy as jnp
from jax import lax
from jax.experimental import pallas as pl
from jax.experimental.pallas import tpu as pltpu

F32 = jnp.float32
BF16 = jnp.bfloat16
I32 = jnp.int32

N_HEADS = 16
N_GROUPS = 4
NORM_EPS = 1e-6

LANES = 128
SUBLANES = 8
MXU_DIM = 256
VMEM_LIMIT = 56 * 1024 * 1024

ROW_TILE = 512
ATT_BLOCK = MXU_DIM
CONV_TILE = 256
CONV_HALO = 32
EXPERT_BLOCK = 256
ROUTE_TILE = 512
MOVE_TILE = 512
META_LANES = LANES
N_CLASS_PAD = 32


def _cparams(sem):
    return pltpu.CompilerParams(dimension_semantics=sem, vmem_limit_bytes=VMEM_LIMIT)


def _sigmoid(x):
    return 1.0 / (1.0 + jnp.exp(-x))


def _norm_mod(x, g, shift, scale):
    ms = jnp.mean(x * x, axis=-1, keepdims=True)
    y = x * lax.rsqrt(ms + NORM_EPS) * g
    return y * (1.0 + scale) + shift


def _ada_kernel(c_ref, w_ref, b_ref, o_ref):
    c = c_ref[...]
    cond = (c * _sigmoid(c)).astype(BF16)
    o_ref[...] = jnp.dot(cond, w_ref[...].astype(BF16), preferred_element_type=F32) + b_ref[...]


def _ada(c, w_ada, b_ada):
    depth, d, n = w_ada.shape
    b = c.shape[0]
    tn = 1536
    return pl.pallas_call(
        _ada_kernel,
        grid=(depth, n // tn),
        in_specs=[pl.BlockSpec((b, d), lambda l, j: (0, 0)),
                  pl.BlockSpec((None, d, tn), lambda l, j: (l, 0, j)),
                  pl.BlockSpec((None, 1, tn), lambda l, j: (l, 0, j))],
        out_specs=pl.BlockSpec((None, b, tn), lambda l, j: (l, 0, j)),
        out_shape=jax.ShapeDtypeStruct((depth, b, n), F32),
        compiler_params=_cparams(("parallel", "parallel")),
        name="ada",
    )(c, w_ada, b_ada.reshape(depth, 1, n))


def _qkv_kernel(x_ref, g_ref, mod_ref, w_ref, o_ref):
    h = _norm_mod(x_ref[...], g_ref[...], mod_ref[0:1, :], mod_ref[1:2, :]).astype(BF16)
    n = o_ref.shape[1]
    tn = 1024
    for j in range(n // tn):
        o_ref[:, j * tn:(j + 1) * tn] = jnp.dot(
            h, w_ref[:, j * tn:(j + 1) * tn], preferred_element_type=F32).astype(o_ref.dtype)


def _qkv_proj(x, g, mod_l, w, seq):
    t, d = x.shape
    n = w.shape[1]
    tm = ROW_TILE
    per_seq = seq // tm
    return pl.pallas_call(
        _qkv_kernel,
        grid=(t // tm,),
        in_specs=[pl.BlockSpec((tm, d), lambda i: (i, 0)),
                  pl.BlockSpec((1, d), lambda i: (0, 0)),
                  pl.BlockSpec((None, 6, d), lambda i: (i // per_seq, 0, 0)),
                  pl.BlockSpec((d, n), lambda i: (0, 0))],
        out_specs=pl.BlockSpec((tm, n), lambda i: (i, 0)),
        out_shape=jax.ShapeDtypeStruct((t, n), BF16),
        compiler_params=_cparams(("parallel",)),
        name="sb_qkv",
    )(x, g.reshape(1, d), mod_l, w)


def _oproj_kernel(o_ref, w_ref, x_ref, mod_ref, out_ref):
    y = jnp.dot(o_ref[...], w_ref[...], preferred_element_type=F32)
    out_ref[...] = x_ref[...] + mod_ref[2:3, :] * y


def _out_proj(o, w, x, mod_l, seq):
    t, d = x.shape
    tm = ROW_TILE
    per_seq = seq // tm
    return pl.pallas_call(
        _oproj_kernel,
        grid=(t // tm,),
        in_specs=[pl.BlockSpec((tm, d), lambda i: (i, 0)),
                  pl.BlockSpec((d, d), lambda i: (0, 0)),
                  pl.BlockSpec((tm, d), lambda i: (i, 0)),
                  pl.BlockSpec((None, 6, d), lambda i: (i // per_seq, 0, 0))],
        out_specs=pl.BlockSpec((tm, d), lambda i: (i, 0)),
        out_shape=jax.ShapeDtypeStruct((t, d), F32),
        compiler_params=_cparams(("parallel",)),
        name="sb_out",
    )(o, w, x, mod_l)


def _sb_head_block(qh, k2, v2, upper, r_ref, o_ref, mask):
    z = lax.dot_general(qh, k2, (((1,), (1,)), ((), ())), preferred_element_type=F32)
    lp = jnp.log(1.0 + jnp.exp(-jnp.abs(z)))
    log_beta = jnp.minimum(z, 0.0) - lp
    log_1m = log_beta - z
    if mask is not None:
        log_1m = jnp.where(mask, log_1m, 0.0)
    inner = jnp.dot(log_1m.astype(BF16), upper, preferred_element_type=F32)
    r = r_ref[...]
    a = jnp.exp(log_beta + inner + r)
    if mask is not None:
        a = jnp.where(mask, a, 0.0)
    o_ref[...] += jnp.dot(a.astype(BF16), v2, preferred_element_type=F32)
    r_ref[...] = r + jnp.sum(log_1m, axis=-1, keepdims=True)


def _sb_attn_kernel(q_ref, k_ref, v_ref, o_ref, r_scr, o_scr):
    seq = q_ref.shape[0]
    tb = ATT_BLOCK
    half = LANES // 2
    head0 = lax.broadcasted_iota(I32, (1, LANES), 1) < half
    row = lax.broadcasted_iota(I32, (tb, tb), 0)
    col = lax.broadcasted_iota(I32, (tb, tb), 1)
    causal = col < row
    upper = (row > col).astype(BF16)

    def q_block(qi, carry):
        q0 = pl.multiple_of(qi * tb, tb)
        q2 = q_ref[pl.ds(q0, tb), :] * 0.125
        qs = (jnp.where(head0, q2, jnp.zeros_like(q2)), jnp.where(head0, jnp.zeros_like(q2), q2))
        r_scr[...] = jnp.zeros_like(r_scr)
        o_scr[...] = jnp.zeros_like(o_scr)

        def kv_block(kb, mask):
            k0 = pl.multiple_of(kb * tb, tb)
            k2 = k_ref[pl.ds(k0, tb), :]
            v2 = v_ref[pl.ds(k0, tb), :]
            for h in range(2):
                _sb_head_block(qs[h], k2, v2, upper, r_scr.at[h], o_scr.at[h], mask)

        kv_block(qi, causal)

        def below(i, c):
            kv_block(qi - 1 - i, None)
            return c

        lax.fori_loop(0, qi, below, 0)
        o_ref[pl.ds(q0, tb), :] = jnp.where(head0, o_scr[0], o_scr[1]).astype(o_ref.dtype)
        return carry

    lax.fori_loop(0, seq // tb, q_block, 0)


def _sb_attention(qkv, batch, seq, d):
    t = batch * seq
    pairs = d // LANES
    tb = ATT_BLOCK
    return pl.pallas_call(
        _sb_attn_kernel,
        grid=(batch, pairs),
        in_specs=[pl.BlockSpec((seq, LANES), lambda b, p: (b, p)),
                  pl.BlockSpec((seq, LANES), lambda b, p: (b, pairs + p)),
                  pl.BlockSpec((seq, LANES), lambda b, p: (b, 2 * pairs + p))],
        out_specs=pl.BlockSpec((seq, LANES), lambda b, p: (b, p)),
        out_shape=jax.ShapeDtypeStruct((t, d), BF16),
        scratch_shapes=[pltpu.VMEM((2, tb, 1), F32), pltpu.VMEM((2, tb, LANES), F32)],
        compiler_params=_cparams(("parallel", "parallel")),
        name="sb_attn",
    )(qkv, qkv, qkv)


def _cv_in_kernel(x_ref, g_ref, mod_ref, w_ref, b_ref, o_ref):
    d = x_ref.shape[1]
    h = _norm_mod(x_ref[...], g_ref[...], mod_ref[0:1, :], mod_ref[1:2, :]).astype(BF16)
    a = jnp.dot(h, w_ref[:, :d], preferred_element_type=F32) + b_ref[:, :d]
    gate = jnp.dot(h, w_ref[:, d:], preferred_element_type=F32) + b_ref[:, d:]
    o_ref[...] = a * _sigmoid(gate)


def _cv_in(x, g, mod_l, w, b, seq):
    t, d = x.shape
    tm = ROW_TILE
    per_seq = seq // tm
    return pl.pallas_call(
        _cv_in_kernel,
        grid=(t // tm,),
        in_specs=[pl.BlockSpec((tm, d), lambda i: (i, 0)),
                  pl.BlockSpec((1, d), lambda i: (0, 0)),
                  pl.BlockSpec((None, 6, d), lambda i: (i // per_seq, 0, 0)),
                  pl.BlockSpec((d, 2 * d), lambda i: (0, 0)),
                  pl.BlockSpec((1, 2 * d), lambda i: (0, 0))],
        out_specs=pl.BlockSpec((tm, d), lambda i: (i, 0)),
        out_shape=jax.ShapeDtypeStruct((t, d), F32),
        compiler_params=_cparams(("parallel",)),
        name="cv_in",
    )(x, g.reshape(1, d), mod_l, w, b.reshape(1, 2 * d))


def _cv_out_kernel(u_ref, halo_ref, wdw_ref, bdw_ref, lng_ref, lnb_ref, w_ref, b_ref, x_ref, mod_ref,
                   out_ref, ext_scr, *, per_seq):
    tm, d = u_ref.shape
    width = wdw_ref.shape[0]
    first = (pl.program_id(0) % per_seq) == 0
    halo = halo_ref[...]
    ext_scr[0:CONV_HALO, :] = jnp.where(first, jnp.zeros_like(halo), halo)
    ext_scr[CONV_HALO:, :] = u_ref[...]
    lead = CONV_HALO - (width - 1)
    acc = jnp.zeros((tm, d), F32)
    for s in range(SUBLANES):
        taps = [k for k in range(width) if (lead + k) % SUBLANES == s]
        if not taps:
            continue
        span = max((lead + k) // SUBLANES for k in taps) * SUBLANES + tm
        shifted = ext_scr[pl.ds(s, span), :]
        for k in taps:
            m = (lead + k) // SUBLANES * SUBLANES
            acc = acc + shifted[m:m + tm, :] * wdw_ref[k:k + 1, :]
    v = acc + bdw_ref[...]
    mu = jnp.mean(v, axis=-1, keepdims=True)
    var = jnp.mean(jnp.square(v - mu), axis=-1, keepdims=True)
    y = (v - mu) * lax.rsqrt(var + NORM_EPS) * lng_ref[...] + lnb_ref[...]
    y = (y * _sigmoid(y)).astype(BF16)
    z = jnp.dot(y, w_ref[...], preferred_element_type=F32) + b_ref[...]
    out_ref[...] = x_ref[...] + mod_ref[2:3, :] * z


def _cv_out(u, w_dw, b_dw, ln_g, ln_b, w, b, x, mod_l, seq):
    t, d = x.shape
    tm = CONV_TILE
    per_seq = seq // tm
    width = w_dw.shape[0]
    assert width - 1 <= CONV_HALO and tm % CONV_HALO == 0
    halo_per_tile = tm // CONV_HALO
    row = lambda i: (0, 0)
    return pl.pallas_call(
        functools.partial(_cv_out_kernel, per_seq=per_seq),
        grid=(t // tm,),
        in_specs=[pl.BlockSpec((tm, d), lambda i: (i, 0)),
                  pl.BlockSpec((CONV_HALO, d), lambda i: (jnp.maximum(i * halo_per_tile - 1, 0), 0)),
                  pl.BlockSpec((width, d), row),
                  pl.BlockSpec((1, d), row), pl.BlockSpec((1, d), row), pl.BlockSpec((1, d), row),
                  pl.BlockSpec((d, d), row), pl.BlockSpec((1, d), row),
                  pl.BlockSpec((tm, d), lambda i: (i, 0)),
                  pl.BlockSpec((None, 6, d), lambda i: (i // per_seq, 0, 0))],
        out_specs=pl.BlockSpec((tm, d), lambda i: (i, 0)),
        out_shape=jax.ShapeDtypeStruct((t, d), F32),
        scratch_shapes=[pltpu.VMEM((tm + CONV_HALO, d), F32)],
        compiler_params=_cparams(("parallel",)),
        name="cv_out",
    )(u, u, w_dw, b_dw.reshape(1, d), ln_g.reshape(1, d), ln_b.reshape(1, d), w, b.reshape(1, d), x, mod_l)


def _top2_of4(v):
    m0, i0 = v[0], jnp.zeros(v[0].shape, I32)
    for j in range(1, 4):
        gt = v[j] > m0
        m0 = jnp.where(gt, v[j], m0)
        i0 = jnp.where(gt, j, i0)
    m1 = jnp.full(v[0].shape, -jnp.inf, F32)
    i1 = jnp.zeros(v[0].shape, I32)
    for j in range(4):
        gt = jnp.logical_and(i0 != j, v[j] > m1)
        m1 = jnp.where(gt, v[j], m1)
        i1 = jnp.where(gt, j, i1)
    return i0, i1


def _pick4(i, v):
    return jnp.where(i == 0, v[0], jnp.where(i == 1, v[1], jnp.where(i == 2, v[2], v[3])))


def _router_kernel(x_ref, g_ref, mod_ref, wr_ref, br_ref, h_ref, mi_ref, cnt_ref, base_scr):
    tm, d = x_ref.shape
    n_e = wr_ref.shape[0]
    epg = n_e // N_GROUPS

    @pl.when(pl.program_id(0) == 0)
    def _():
        base_scr[...] = jnp.zeros_like(base_scr)

    h = _norm_mod(x_ref[...], g_ref[...], mod_ref[3:4, :], mod_ref[4:5, :])
    h_ref[:, :d] = h
    logits = lax.dot_general(wr_ref[...], h.astype(BF16), (((1,), (1,)), ((), ())),
                             preferred_element_type=F32)
    scores = _sigmoid(logits)
    sel = scores + br_ref[...]
    sel_r = [sel[e:e + 1, :] for e in range(n_e)]
    sc_r = [scores[e:e + 1, :] for e in range(n_e)]

    best, gi = None, None
    for g in range(N_GROUPS):
        a, b, c, e4 = sel_r[g * epg:(g + 1) * epg]
        hi1, lo1 = jnp.maximum(a, b), jnp.minimum(a, b)
        hi2, lo2 = jnp.maximum(c, e4), jnp.minimum(c, e4)
        top1 = jnp.maximum(hi1, hi2)
        top2 = jnp.maximum(jnp.minimum(hi1, hi2), jnp.maximum(lo1, lo2))
        gs = top1 + top2
        if g == 0:
            best, gi = gs, jnp.zeros(gs.shape, I32)
        else:
            gt = gs > best
            best = jnp.where(gt, gs, best)
            gi = jnp.where(gt, g, gi)

    in_sel = [_pick4(gi, [sel_r[g * epg + j] for g in range(N_GROUPS)]) for j in range(epg)]
    in_sc = [_pick4(gi, [sc_r[g * epg + j] for g in range(N_GROUPS)]) for j in range(epg)]
    i0, i1 = _top2_of4(in_sel)
    w0, w1 = _pick4(i0, in_sc), _pick4(i1, in_sc)
    wsum = w0 + w1
    w0, w1 = w0 / wsum, w1 / wsum

    lo, hi = jnp.minimum(i0, i1), jnp.maximum(i0, i1)
    pair = jnp.where(lo == 0, 0, jnp.where(lo == 1, 3, 5)) + hi - lo - 1
    cls = gi * 6 + pair
    wa = jnp.where(i0 < i1, w0, w1)
    wb = jnp.where(i0 < i1, w1, w0)

    onehot = (lax.broadcasted_iota(I32, (N_CLASS_PAD, tm), 0) == cls).astype(BF16)
    before = (lax.broadcasted_iota(I32, (tm, tm), 0) < lax.broadcasted_iota(I32, (tm, tm), 1)).astype(BF16)
    prefix = jnp.dot(onehot, before, preferred_element_type=F32)
    base = base_scr[...]
    oh = onehot.astype(F32)
    rank = jnp.sum(oh * (prefix + base), axis=0, keepdims=True)
    base = base + jnp.sum(oh, axis=1, keepdims=True)
    base_scr[...] = base
    cnt_ref[...] = jnp.broadcast_to(base, cnt_ref.shape)

    mi_ref[...] = jnp.concatenate(
        [cls, rank.astype(I32), jnp.zeros((SUBLANES - 2, tm), I32)], axis=0)
    wrows = jnp.concatenate([wa, wb, jnp.zeros((META_LANES - 2, tm), F32)], axis=0)
    h_ref[:, d:] = wrows.T


def _router(x, g, mod_l, w_r_t, b_r, seq):
    t, d = x.shape
    n_e = w_r_t.shape[0]
    tm = ROUTE_TILE
    per_seq = seq // tm
    return pl.pallas_call(
        _router_kernel,
        grid=(t // tm,),
        in_specs=[pl.BlockSpec((tm, d), lambda i: (i, 0)),
                  pl.BlockSpec((1, d), lambda i: (0, 0)),
                  pl.BlockSpec((None, 6, d), lambda i: (i // per_seq, 0, 0)),
                  pl.BlockSpec((n_e, d), lambda i: (0, 0)),
                  pl.BlockSpec((n_e, 1), lambda i: (0, 0))],
        out_specs=[pl.BlockSpec((tm, d + META_LANES), lambda i: (i, 0)),
                   pl.BlockSpec((SUBLANES, tm), lambda i: (0, i)),
                   pl.BlockSpec((N_CLASS_PAD, LANES), lambda i: (0, 0))],
        out_shape=[jax.ShapeDtypeStruct((t, d + META_LANES), F32),
                   jax.ShapeDtypeStruct((SUBLANES, t), I32),
                   jax.ShapeDtypeStruct((N_CLASS_PAD, LANES), F32)],
        scratch_shapes=[pltpu.VMEM((N_CLASS_PAD, 1), F32)],
        compiler_params=_cparams(("arbitrary",)),
        name="moe_router",
    )(x, g.reshape(1, d), mod_l, w_r_t, b_r.reshape(n_e, 1))


def _row_copy(src, s, dst, r, sem):
    return pltpu.make_async_copy(src.at[pl.ds(s, 1)], dst.at[pl.ds(r, 1)], sem)


def _dispatch_kernel(zero_flag_ref, dest_ref, h_hbm, xs_hbm, zero_scr, sem, zsem):
    tm = dest_ref.shape[1]
    blk = zero_scr.shape[0]
    n_blk = xs_hbm.shape[0] // blk
    i = pl.program_id(0)

    @pl.when(i == 0)
    def _():
        zero_scr[...] = jnp.zeros_like(zero_scr)

        def fill(j, n):
            flag = zero_flag_ref[j]

            @pl.when(flag == 1)
            def _():
                pltpu.make_async_copy(zero_scr, xs_hbm.at[pl.ds(pl.multiple_of(j * blk, blk), blk)], zsem).start()

            return n + flag

        n_fill = lax.fori_loop(0, n_blk, fill, 0)

        def drain(j, c):
            pltpu.make_async_copy(zero_scr, xs_hbm.at[pl.ds(0, blk)], zsem).wait()
            return c

        lax.fori_loop(0, n_fill, drain, 0)

    t0 = i * tm

    def issue(r, c):
        _row_copy(h_hbm, t0 + r, xs_hbm, dest_ref[0, r], sem).start()
        return c

    lax.fori_loop(0, tm, issue, 0, unroll=8)
    pltpu.make_async_copy(h_hbm.at[pl.ds(0, tm)], xs_hbm.at[pl.ds(0, tm)], sem).wait()


def _dispatch(zero_flag, dest, h_ext, n_rows):
    t, width = h_ext.shape
    tm = MOVE_TILE
    grid_spec = pltpu.PrefetchScalarGridSpec(
        num_scalar_prefetch=1,
        grid=(t // tm,),
        in_specs=[pl.BlockSpec((None, 1, tm), lambda i, zf: (i, 0, 0), memory_space=pltpu.SMEM),
                  pl.BlockSpec(memory_space=pl.ANY)],
        out_specs=pl.BlockSpec(memory_space=pl.ANY),
        scratch_shapes=[pltpu.VMEM((EXPERT_BLOCK, width), F32),
                        pltpu.SemaphoreType.DMA(()), pltpu.SemaphoreType.DMA(())],
    )
    return pl.pallas_call(
        _dispatch_kernel,
        grid_spec=grid_spec,
        out_shape=jax.ShapeDtypeStruct((n_rows, width), F32),
        compiler_params=_cparams(("arbitrary",)),
        name="moe_dispatch",
    )(zero_flag, dest.reshape(t // tm, 1, tm), h_ext)


def _swiglu(x, w1_ref, wd_ref):
    f = wd_ref.shape[0]
    gu = jnp.dot(x, w1_ref[...], preferred_element_type=F32)
    gate, up = gu[:, :f], gu[:, f:]
    hidden = (gate * _sigmoid(gate) * up).astype(BF16)
    return jnp.dot(hidden, wd_ref[...], preferred_element_type=F32)


def _expert_kernel(ea_ref, eb_ref, used_ref, xs_ref, w1a_ref, wda_ref, w1b_ref, wdb_ref, y_ref):
    d = y_ref.shape[1]
    j = pl.program_id(0)

    @pl.when(j < used_ref[0])
    def _():
        x = xs_ref[:, :d].astype(BF16)
        wa = xs_ref[:, d:d + 1]
        wb = xs_ref[:, d + 1:d + 2]
        y_ref[...] = wa * _swiglu(x, w1a_ref, wda_ref) + wb * _swiglu(x, w1b_ref, wdb_ref)

    @pl.when(j >= used_ref[0])
    def _():
        y_ref[...] = jnp.zeros_like(y_ref)


def _experts(blk_ea, blk_eb, n_used, xs, w1, wd):
    n_rows, width = xs.shape
    n_e, d, f2 = w1.shape
    f = f2 // 2
    blk = EXPERT_BLOCK
    grid_spec = pltpu.PrefetchScalarGridSpec(
        num_scalar_prefetch=3,
        grid=(n_rows // blk,),
        in_specs=[pl.BlockSpec((blk, width), lambda j, ea, eb, nu: (j, 0)),
                  pl.BlockSpec((None, d, f2), lambda j, ea, eb, nu: (ea[j], 0, 0)),
                  pl.BlockSpec((None, f, d), lambda j, ea, eb, nu: (ea[j], 0, 0)),
                  pl.BlockSpec((None, d, f2), lambda j, ea, eb, nu: (eb[j], 0, 0)),
                  pl.BlockSpec((None, f, d), lambda j, ea, eb, nu: (eb[j], 0, 0))],
        out_specs=pl.BlockSpec((blk, d), lambda j, ea, eb, nu: (j, 0)),
    )
    return pl.pallas_call(
        _expert_kernel,
        grid_spec=grid_spec,
        out_shape=jax.ShapeDtypeStruct((n_rows, d), F32),
        compiler_params=_cparams(("arbitrary",)),
        name="moe_experts",
    )(blk_ea, blk_eb, n_used, xs, w1, wd, w1, wd)


def _combine_kernel(dest_ref, y_hbm, x_ref, mod_ref, fg_ref, out_ref, y_scr, sem, *, final):
    tm = x_ref.shape[0]

    def issue(r, c):
        _row_copy(y_hbm, dest_ref[0, r], y_scr, r, sem).start()
        return c

    lax.fori_loop(0, tm, issue, 0, unroll=8)
    pltpu.make_async_copy(y_hbm.at[pl.ds(0, tm)], y_scr, sem).wait()
    x = x_ref[...] + mod_ref[5:6, :] * y_scr[...]
    if final:
        ms = jnp.mean(x * x, axis=-1, keepdims=True)
        x = x * lax.rsqrt(ms + NORM_EPS) * fg_ref[...]
    out_ref[...] = x


def _combine(dest, y, x, mod_l, final_g, seq, final):
    t, d = x.shape
    tm = MOVE_TILE
    per_seq = seq // tm
    return pl.pallas_call(
        functools.partial(_combine_kernel, final=final),
        grid=(t // tm,),
        in_specs=[pl.BlockSpec((None, 1, tm), lambda i: (i, 0, 0), memory_space=pltpu.SMEM),
                  pl.BlockSpec(memory_space=pl.ANY),
                  pl.BlockSpec((tm, d), lambda i: (i, 0)),
                  pl.BlockSpec((None, 6, d), lambda i: (i // per_seq, 0, 0)),
                  pl.BlockSpec((1, d), lambda i: (0, 0))],
        out_specs=pl.BlockSpec((tm, d), lambda i: (i, 0)),
        out_shape=jax.ShapeDtypeStruct((t, d), F32),
        scratch_shapes=[pltpu.VMEM((tm, d), F32), pltpu.SemaphoreType.DMA(())],
        compiler_params=_cparams(("arbitrary",)),
        name="moe_combine",
    )(dest.reshape(t // tm, 1, tm), y, x, mod_l, final_g.reshape(1, d))


def _moe_plan(meta_i, counts, n_e):
    n_cls = N_GROUPS * 6
    epg = n_e // N_GROUPS
    blk = EXPERT_BLOCK
    t = meta_i.shape[1]
    n_blk = t // blk + n_cls
    cls, rank = meta_i[0], meta_i[1]
    cnt = counts[:n_cls, 0].astype(I32)
    padded = (cnt + blk - 1) // blk * blk
    pad_end = jnp.cumsum(padded)
    pad_start = pad_end - padded
    dest = jnp.sum(jnp.where(cls[None, :] == jnp.arange(n_cls, dtype=I32)[:, None], pad_start[:, None], 0),
                   axis=0).astype(I32) + rank
    n_used = (pad_end[-1] // blk).astype(I32)
    blk_start = jnp.arange(n_blk, dtype=I32) * blk
    blk_cls = jnp.minimum(jnp.sum(blk_start[:, None] >= pad_end[None, :], axis=1), n_cls - 1).astype(I32)
    pair_lo = jnp.array([0, 0, 0, 1, 1, 2], I32)
    pair_hi = jnp.array([1, 2, 3, 2, 3, 3], I32)
    blk_ea = (blk_cls // 6) * epg + pair_lo[blk_cls % 6]
    blk_eb = (blk_cls // 6) * epg + pair_hi[blk_cls % 6]
    last_blk = pad_end // blk - 1
    partial = jnp.logical_and(cnt % blk != 0, cnt > 0)
    is_last_partial = jnp.any(jnp.logical_and(jnp.arange(n_blk, dtype=I32)[:, None] == last_blk[None, :],
                                              partial[None, :]), axis=1)
    zero_flag = jnp.logical_or(is_last_partial, jnp.arange(n_blk, dtype=I32) >= n_used).astype(I32)
    return dest, blk_ea.astype(I32), blk_eb.astype(I32), n_used.reshape(1), zero_flag, n_blk * blk


def _moe(x, g, mod_l, w_r_t, b_r, w1, wd, final_g, seq, final):
    n_e = w_r_t.shape[0]
    h_ext, meta_i, counts = _router(x, g, mod_l, w_r_t, b_r, seq)
    dest, blk_ea, blk_eb, n_used, zero_flag, n_rows = _moe_plan(meta_i, counts, n_e)
    xs = _dispatch(zero_flag, dest, h_ext, n_rows)
    y = _experts(blk_ea, blk_eb, n_used, xs, w1, wd)
    return _combine(dest, y, x, mod_l, final_g, seq, final)


def kernel(x, c, norm1_g, norm2_g, w_ada, b_ada, sb_w_qkv, sb_w_o, cv_w_in, cv_b_in, cv_w_dw, cv_b_dw, cv_ln_g, cv_ln_b, cv_w_out, cv_b_out, w_router, b_router, moe_w_gate, moe_w_up, moe_w_down, final_g):
    batch, seq, d = x.shape
    depth = w_ada.shape[0]
    assert d == N_HEADS * (LANES // 2) and seq % ROW_TILE == 0 and seq % ATT_BLOCK == 0
    assert (batch * seq) % EXPERT_BLOCK == 0 and w_router.shape[1] == 4 * N_GROUPS

    mod = _ada(c, w_ada, b_ada).reshape(depth, batch, 6, d)
    w_r_t = w_router.T.astype(BF16)
    xf = x.reshape(batch * seq, d)
    for layer in range(depth):
        mod_l = mod[layer]
        j = layer // 2
        if layer % 2 == 0:
            qkv = _qkv_proj(xf, norm1_g[layer], mod_l, sb_w_qkv[j].astype(BF16), seq)
            o = _sb_attention(qkv, batch, seq, d)
            xf = _out_proj(o, sb_w_o[j].astype(BF16), xf, mod_l, seq)
        else:
            u = _cv_in(xf, norm1_g[layer], mod_l, cv_w_in[j].astype(BF16), cv_b_in[j], seq)
            xf = _cv_out(u, cv_w_dw[j], cv_b_dw[j], cv_ln_g[j], cv_ln_b[j],
                         cv_w_out[j].astype(BF16), cv_b_out[j], xf, mod_l, seq)
        w1 = jnp.concatenate([moe_w_gate[layer], moe_w_up[layer]], axis=-1).astype(BF16)
        wd = moe_w_down[layer].astype(BF16)
        xf = _moe(xf, norm2_g[layer], mod_l, w_r_t, b_router, w1, wd, final_g, seq, layer == depth - 1)
    return xf.reshape(batch, seq, d)
```

```python
import functools

import jax
import jax.numpy as jnp
from jax import lax
from jax.experimental import pallas as pl
from jax.experimental.pallas import tpu as pltpu

F32 = jnp.float32
BF16 = jnp.bfloat16
I32 = jnp.int32
U32 = jnp.uint32

N_HEADS = 16
N_GROUPS = 4
NORM_EPS = 1e-6

LANES = 128
SUBLANES = 8
MXU_DIM = 256
VMEM_LIMIT = 56 * 1024 * 1024

ROW_TILE = 512
ATT_BLOCK = MXU_DIM
ATT_PAIRS = 4
CONV_TILE = 256
CONV_HALO = 32
CONV_CHUNK = 128
EXPERT_BLOCK = 256
ROUTE_TILE = 512
MOVE_TILE = 1024
N_CLASS_PAD = 32


def _cparams(sem):
    return pltpu.CompilerParams(dimension_semantics=sem, vmem_limit_bytes=VMEM_LIMIT)


def _sigmoid(x):
    return 1.0 / (1.0 + jnp.exp(-x))


def _norm_mod(x, g, shift, scale):
    ms = jnp.mean(x * x, axis=-1, keepdims=True)
    y = x * lax.rsqrt(ms + NORM_EPS) * g
    return y * (1.0 + scale) + shift


def _ada_kernel(c_ref, w_ref, b_ref, o_ref):
    c = c_ref[...]
    cond = (c * _sigmoid(c)).astype(BF16)
    o_ref[...] = jnp.dot(cond, w_ref[...].astype(BF16), preferred_element_type=F32) + b_ref[...]


def _ada(c, w_ada, b_ada):
    depth, d, n = w_ada.shape
    b = c.shape[0]
    tn = 1536
    return pl.pallas_call(
        _ada_kernel,
        grid=(depth, n // tn),
        in_specs=[pl.BlockSpec((b, d), lambda l, j: (0, 0)),
                  pl.BlockSpec((None, d, tn), lambda l, j: (l, 0, j)),
                  pl.BlockSpec((None, 1, tn), lambda l, j: (l, 0, j))],
        out_specs=pl.BlockSpec((None, b, tn), lambda l, j: (l, 0, j)),
        out_shape=jax.ShapeDtypeStruct((depth, b, n), F32),
        compiler_params=_cparams(("parallel", "parallel")),
        name="ada",
    )(c, w_ada, b_ada.reshape(depth, 1, n))


def _qkv_kernel(x_ref, g_ref, mod_ref, w_ref, o_ref):
    h = _norm_mod(x_ref[...], g_ref[...], mod_ref[0:1, :], mod_ref[1:2, :]).astype(BF16)
    n = o_ref.shape[1]
    tn = 1024
    for j in range(n // tn):
        o_ref[:, j * tn:(j + 1) * tn] = jnp.dot(
            h, w_ref[:, j * tn:(j + 1) * tn], preferred_element_type=F32).astype(o_ref.dtype)


def _qkv_proj(x, g, mod_l, w, seq):
    t, d = x.shape
    n = w.shape[1]
    tm = ROW_TILE
    per_seq = seq // tm
    return pl.pallas_call(
        _qkv_kernel,
        grid=(t // tm,),
        in_specs=[pl.BlockSpec((tm, d), lambda i: (i, 0)),
                  pl.BlockSpec((1, d), lambda i: (0, 0)),
                  pl.BlockSpec((None, 6, d), lambda i: (i // per_seq, 0, 0)),
                  pl.BlockSpec((d, n), lambda i: (0, 0))],
        out_specs=pl.BlockSpec((tm, n), lambda i: (i, 0)),
        out_shape=jax.ShapeDtypeStruct((t, n), BF16),
        compiler_params=_cparams(("parallel",)),
        name="sb_qkv",
    )(x, g.reshape(1, d), mod_l, w)


def _oproj_kernel(o_ref, w_ref, x_ref, mod_ref, out_ref):
    y = jnp.dot(o_ref[...], w_ref[...], preferred_element_type=F32)
    out_ref[...] = x_ref[...] + mod_ref[2:3, :] * y


def _out_proj(o, w, x, mod_l, seq):
    t, d = x.shape
    tm = ROW_TILE
    per_seq = seq // tm
    return pl.pallas_call(
        _oproj_kernel,
        grid=(t // tm,),
        in_specs=[pl.BlockSpec((tm, d), lambda i: (i, 0)),
                  pl.BlockSpec((d, d), lambda i: (0, 0)),
                  pl.BlockSpec((tm, d), lambda i: (i, 0)),
                  pl.BlockSpec((None, 6, d), lambda i: (i // per_seq, 0, 0))],
        out_specs=pl.BlockSpec((tm, d), lambda i: (i, 0)),
        out_shape=jax.ShapeDtypeStruct((t, d), F32),
        compiler_params=_cparams(("parallel",)),
        name="sb_out",
    )(o, w, x, mod_l)


def _sb_head_block(qh, k2, v2, upper, r_ref, o_ref, mask):
    z = lax.dot_general(qh, k2, (((1,), (1,)), ((), ())), preferred_element_type=F32)
    lp = jnp.log(1.0 + jnp.exp(-jnp.abs(z)))
    log_beta = jnp.minimum(z, 0.0) - lp
    log_1m = log_beta - z
    if mask is not None:
        log_1m = jnp.where(mask, log_1m, 0.0)
    inner = jnp.dot(log_1m.astype(BF16), upper, preferred_element_type=F32)
    r = r_ref[...]
    a = jnp.exp(log_beta + inner + r)
    if mask is not None:
        a = jnp.where(mask, a, 0.0)
    o_ref[...] += jnp.dot(a.astype(BF16), v2, preferred_element_type=F32)
    r_ref[...] = r + jnp.sum(log_1m, axis=-1, keepdims=True)


def _sb_attn_kernel(q_ref, k_ref, v_ref, o_ref, r_scr, o_scr):
    seq = q_ref.shape[0]
    tb = ATT_BLOCK
    half = LANES // 2
    head0 = lax.broadcasted_iota(I32, (1, LANES), 1) < half
    row = lax.broadcasted_iota(I32, (tb, tb), 0)
    col = lax.broadcasted_iota(I32, (tb, tb), 1)
    causal = col < row
    upper = (row > col).astype(BF16)

    def q_block(qi, carry):
        q0 = pl.multiple_of(qi * tb, tb)
        qs = []
        for p in range(ATT_PAIRS):
            q2 = q_ref[pl.ds(q0, tb), p * LANES:(p + 1) * LANES] * 0.125
            qs.append((jnp.where(head0, q2, jnp.zeros_like(q2)), jnp.where(head0, jnp.zeros_like(q2), q2)))
        r_scr[...] = jnp.zeros_like(r_scr)
        o_scr[...] = jnp.zeros_like(o_scr)

        def kv_block(kb, mask):
            k0 = pl.multiple_of(kb * tb, tb)
            for p in range(ATT_PAIRS):
                lanes = slice(p * LANES, (p + 1) * LANES)
                k2 = k_ref[pl.ds(k0, tb), lanes]
                v2 = v_ref[pl.ds(k0, tb), lanes]
                for h in range(2):
                    _sb_head_block(qs[p][h], k2, v2, upper, r_scr.at[2 * p + h], o_scr.at[2 * p + h], mask)

        kv_block(qi, causal)

        def below(i, c):
            kv_block(qi - 1 - i, None)
            return c

        lax.fori_loop(0, qi, below, 0)
        for p in range(ATT_PAIRS):
            o_ref[pl.ds(q0, tb), p * LANES:(p + 1) * LANES] = jnp.where(
                head0, o_scr[2 * p], o_scr[2 * p + 1]).astype(o_ref.dtype)
        return carry

    lax.fori_loop(0, seq // tb, q_block, 0)


def _sb_attention(qkv, batch, seq, d):
    t = batch * seq
    width = ATT_PAIRS * LANES
    groups = d // width
    tb = ATT_BLOCK
    return pl.pallas_call(
        _sb_attn_kernel,
        grid=(batch, groups),
        in_specs=[pl.BlockSpec((seq, width), lambda b, p: (b, p)),
                  pl.BlockSpec((seq, width), lambda b, p: (b, groups + p)),
                  pl.BlockSpec((seq, width), lambda b, p: (b, 2 * groups + p))],
        out_specs=pl.BlockSpec((seq, width), lambda b, p: (b, p)),
        out_shape=jax.ShapeDtypeStruct((t, d), BF16),
        scratch_shapes=[pltpu.VMEM((2 * ATT_PAIRS, tb, 1), F32), pltpu.VMEM((2 * ATT_PAIRS, tb, LANES), F32)],
        compiler_params=_cparams(("parallel", "parallel")),
        name="sb_attn",
    )(qkv, qkv, qkv)


def _cv_in_kernel(x_ref, g_ref, mod_ref, w_ref, b_ref, o_ref):
    d = x_ref.shape[1]
    h = _norm_mod(x_ref[...], g_ref[...], mod_ref[0:1, :], mod_ref[1:2, :]).astype(BF16)
    a = jnp.dot(h, w_ref[:, :d], preferred_element_type=F32) + b_ref[:, :d]
    gate = jnp.dot(h, w_ref[:, d:], preferred_element_type=F32) + b_ref[:, d:]
    o_ref[...] = a * _sigmoid(gate)


def _cv_in(x, g, mod_l, w, b, seq):
    t, d = x.shape
    tm = ROW_TILE
    per_seq = seq // tm
    return pl.pallas_call(
        _cv_in_kernel,
        grid=(t // tm,),
        in_specs=[pl.BlockSpec((tm, d), lambda i: (i, 0)),
                  pl.BlockSpec((1, d), lambda i: (0, 0)),
                  pl.BlockSpec((None, 6, d), lambda i: (i // per_seq, 0, 0)),
                  pl.BlockSpec((d, 2 * d), lambda i: (0, 0)),
                  pl.BlockSpec((1, 2 * d), lambda i: (0, 0))],
        out_specs=pl.BlockSpec((tm, d), lambda i: (i, 0)),
        out_shape=jax.ShapeDtypeStruct((t, d), F32),
        compiler_params=_cparams(("parallel",)),
        name="cv_in",
    )(x, g.reshape(1, d), mod_l, w, b.reshape(1, 2 * d))


def _cv_out_kernel(u_ref, halo_ref, wdw_ref, bdw_ref, lng_ref, lnb_ref, w_ref, b_ref, x_ref, mod_ref,
                   out_ref, ext_scr, conv_scr, *, per_seq):
    tm, d = u_ref.shape
    width = wdw_ref.shape[0]
    first = (pl.program_id(0) % per_seq) == 0
    halo = halo_ref[...]
    ext_scr[0:CONV_HALO, :] = jnp.where(first, jnp.zeros_like(halo), halo)
    ext_scr[CONV_HALO:, :] = u_ref[...]
    lead = CONV_HALO - (width - 1)
    rows = CONV_CHUNK
    win = rows + CONV_HALO
    for rc in range(tm // rows):
        for c in range(d // LANES):
            lanes = slice(c * LANES, (c + 1) * LANES)
            window = ext_scr[rc * rows:rc * rows + win, lanes]
            acc = jnp.zeros((rows, LANES), F32)
            for s in range(SUBLANES):
                taps = [k for k in range(width) if (lead + k) % SUBLANES == s]
                if not taps:
                    continue
                shifted = window if s == 0 else pltpu.roll(window, win - s, axis=0)
                for k in taps:
                    m = (lead + k) // SUBLANES * SUBLANES
                    acc = acc + shifted[m:m + rows, :] * wdw_ref[k:k + 1, lanes]
            conv_scr[rc * rows:(rc + 1) * rows, lanes] = acc
    v = conv_scr[...] + bdw_ref[...]
    mu = jnp.mean(v, axis=-1, keepdims=True)
    var = jnp.mean(jnp.square(v - mu), axis=-1, keepdims=True)
    y = (v - mu) * lax.rsqrt(var + NORM_EPS) * lng_ref[...] + lnb_ref[...]
    y = (y * _sigmoid(y)).astype(BF16)
    z = jnp.dot(y, w_ref[...], preferred_element_type=F32) + b_ref[...]
    out_ref[...] = x_ref[...] + mod_ref[2:3, :] * z


def _cv_out(u, w_dw, b_dw, ln_g, ln_b, w, b, x, mod_l, seq):
    t, d = x.shape
    tm = CONV_TILE
    per_seq = seq // tm
    width = w_dw.shape[0]
    assert width - 1 <= CONV_HALO and tm % CONV_HALO == 0
    halo_per_tile = tm // CONV_HALO
    row = lambda i: (0, 0)
    return pl.pallas_call(
        functools.partial(_cv_out_kernel, per_seq=per_seq),
        grid=(t // tm,),
        in_specs=[pl.BlockSpec((tm, d), lambda i: (i, 0)),
                  pl.BlockSpec((CONV_HALO, d), lambda i: (jnp.maximum(i * halo_per_tile - 1, 0), 0)),
                  pl.BlockSpec((width, d), row),
                  pl.BlockSpec((1, d), row), pl.BlockSpec((1, d), row), pl.BlockSpec((1, d), row),
                  pl.BlockSpec((d, d), row), pl.BlockSpec((1, d), row),
                  pl.BlockSpec((tm, d), lambda i: (i, 0)),
                  pl.BlockSpec((None, 6, d), lambda i: (i // per_seq, 0, 0))],
        out_specs=pl.BlockSpec((tm, d), lambda i: (i, 0)),
        out_shape=jax.ShapeDtypeStruct((t, d), F32),
        scratch_shapes=[pltpu.VMEM((tm + CONV_HALO, d), F32), pltpu.VMEM((tm, d), F32)],
        compiler_params=_cparams(("parallel",)),
        name="cv_out",
    )(u, u, w_dw, b_dw.reshape(1, d), ln_g.reshape(1, d), ln_b.reshape(1, d), w, b.reshape(1, d), x, mod_l)


def _top2_of4(v):
    m0, i0 = v[0], jnp.zeros(v[0].shape, I32)
    for j in range(1, 4):
        gt = v[j] > m0
        m0 = jnp.where(gt, v[j], m0)
        i0 = jnp.where(gt, j, i0)
    m1 = jnp.full(v[0].shape, -jnp.inf, F32)
    i1 = jnp.zeros(v[0].shape, I32)
    for j in range(4):
        gt = jnp.logical_and(i0 != j, v[j] > m1)
        m1 = jnp.where(gt, v[j], m1)
        i1 = jnp.where(gt, j, i1)
    return i0, i1


def _pick4(i, v):
    return jnp.where(i == 0, v[0], jnp.where(i == 1, v[1], jnp.where(i == 2, v[2], v[3])))


def _router_kernel(x_ref, g_ref, mod_ref, wr_ref, br_ref, h_ref, mi_ref, cnt_ref, base_scr):
    tm, d = x_ref.shape
    n_e = wr_ref.shape[0]
    epg = n_e // N_GROUPS

    @pl.when(pl.program_id(0) == 0)
    def _():
        base_scr[...] = jnp.zeros_like(base_scr)

    hb = _norm_mod(x_ref[...], g_ref[...], mod_ref[3:4, :], mod_ref[4:5, :]).astype(BF16)
    bits = pltpu.bitcast(hb.astype(F32), U32)
    half = d // 2
    words = (bits[:, half:] & jnp.uint32(0xFFFF0000)) | lax.shift_right_logical(bits[:, :half], jnp.uint32(16))
    for c in range(half // LANES):
        h_ref[pl.ds(c, tm, stride=SUBLANES), :] = words[:, c * LANES:(c + 1) * LANES]
    for c in range(half // LANES + 1, SUBLANES):
        h_ref[pl.ds(c, tm, stride=SUBLANES), :] = jnp.zeros((tm, LANES), U32)
    logits = lax.dot_general(wr_ref[...], hb, (((1,), (1,)), ((), ())),
                             preferred_element_type=F32)
    scores = _sigmoid(logits)
    sel = scores + br_ref[...]
    sel_r = [sel[e:e + 1, :] for e in range(n_e)]
    sc_r = [scores[e:e + 1, :] for e in range(n_e)]

    best, gi = None, None
    for g in range(N_GROUPS):
        a, b, c, e4 = sel_r[g * epg:(g + 1) * epg]
        hi1, lo1 = jnp.maximum(a, b), jnp.minimum(a, b)
        hi2, lo2 = jnp.maximum(c, e4), jnp.minimum(c, e4)
        top1 = jnp.maximum(hi1, hi2)
        top2 = jnp.maximum(jnp.minimum(hi1, hi2), jnp.maximum(lo1, lo2))
        gs = top1 + top2
        if g == 0:
            best, gi = gs, jnp.zeros(gs.shape, I32)
        else:
            gt = gs > best
            best = jnp.where(gt, gs, best)
            gi = jnp.where(gt, g, gi)

    in_sel = [_pick4(gi, [sel_r[g * epg + j] for g in range(N_GROUPS)]) for j in range(epg)]
    in_sc = [_pick4(gi, [sc_r[g * epg + j] for g in range(N_GROUPS)]) for j in range(epg)]
    i0, i1 = _top2_of4(in_sel)
    w0, w1 = _pick4(i0, in_sc), _pick4(i1, in_sc)
    wsum = w0 + w1
    w0, w1 = w0 / wsum, w1 / wsum

    lo, hi = jnp.minimum(i0, i1), jnp.maximum(i0, i1)
    pair = jnp.where(lo == 0, 0, jnp.where(lo == 1, 3, 5)) + hi - lo - 1
    cls = gi * 6 + pair
    wa = jnp.where(i0 < i1, w0, w1)
    wb = jnp.where(i0 < i1, w1, w0)

    onehot = (lax.broadcasted_iota(I32, (N_CLASS_PAD, tm), 0) == cls).astype(BF16)
    before = (lax.broadcasted_iota(I32, (tm, tm), 0) < lax.broadcasted_iota(I32, (tm, tm), 1)).astype(BF16)
    prefix = jnp.dot(onehot, before, preferred_element_type=F32)
    base = base_scr[...]
    oh = onehot.astype(F32)
    rank = jnp.sum(oh * (prefix + base), axis=0, keepdims=True)
    base = base + jnp.sum(oh, axis=1, keepdims=True)
    base_scr[...] = base
    cnt_ref[...] = jnp.broadcast_to(base, cnt_ref.shape)

    mi_ref[...] = jnp.concatenate(
        [cls, rank.astype(I32), jnp.zeros((SUBLANES - 2, tm), I32)], axis=0)
    wrows = jnp.concatenate([wa, wb, jnp.zeros((LANES - 2, tm), F32)], axis=0)
    h_ref[pl.ds(half // LANES, tm, stride=SUBLANES), :] = pltpu.bitcast(wrows.T, U32)


def _router(x, g, mod_l, w_r_t, b_r, seq):
    t, d = x.shape
    n_e = w_r_t.shape[0]
    tm = ROUTE_TILE
    per_seq = seq // tm
    return pl.pallas_call(
        _router_kernel,
        grid=(t // tm,),
        in_specs=[pl.BlockSpec((tm, d), lambda i: (i, 0)),
                  pl.BlockSpec((1, d), lambda i: (0, 0)),
                  pl.BlockSpec((None, 6, d), lambda i: (i // per_seq, 0, 0)),
                  pl.BlockSpec((n_e, d), lambda i: (0, 0)),
                  pl.BlockSpec((n_e, 1), lambda i: (0, 0))],
        out_specs=[pl.BlockSpec((tm * SUBLANES, LANES), lambda i: (i, 0)),
                   pl.BlockSpec((SUBLANES, tm), lambda i: (0, i)),
                   pl.BlockSpec((N_CLASS_PAD, LANES), lambda i: (0, 0))],
        out_shape=[jax.ShapeDtypeStruct((t * SUBLANES, LANES), U32),
                   jax.ShapeDtypeStruct((SUBLANES, t), I32),
                   jax.ShapeDtypeStruct((N_CLASS_PAD, LANES), F32)],
        scratch_shapes=[pltpu.VMEM((N_CLASS_PAD, 1), F32)],
        compiler_params=_cparams(("arbitrary",)),
        name="moe_router",
    )(x, g.reshape(1, d), mod_l, w_r_t, b_r.reshape(n_e, 1))


def _row_copy(src, s, dst, r, sem):
    return pltpu.make_async_copy(src.at[pl.ds(pl.multiple_of(s * SUBLANES, SUBLANES), SUBLANES)],
                                 dst.at[pl.ds(pl.multiple_of(r * SUBLANES, SUBLANES), SUBLANES)], sem)


def _dispatch_kernel(zero_flag_ref, dest_ref, h_ref, xs_hbm, zero_scr, sem, zsem):
    tm = dest_ref.shape[1]
    blk = zero_scr.shape[0]
    n_blk = xs_hbm.shape[0] // blk
    i = pl.program_id(0)

    @pl.when(i == 0)
    def _():
        zero_scr[...] = jnp.zeros_like(zero_scr)

        def fill(j, n):
            flag = zero_flag_ref[j]

            @pl.when(flag == 1)
            def _():
                pltpu.make_async_copy(zero_scr, xs_hbm.at[pl.ds(pl.multiple_of(j * blk, blk), blk)], zsem).start()

            return n + flag

        n_fill = lax.fori_loop(0, n_blk, fill, 0)

        def drain(j, c):
            pltpu.make_async_copy(zero_scr, xs_hbm.at[pl.ds(0, blk)], zsem).wait()
            return c

        lax.fori_loop(0, n_fill, drain, 0)

    def issue(r, c):
        _row_copy(h_ref, r, xs_hbm, dest_ref[0, r], sem).start()
        return c

    lax.fori_loop(0, tm, issue, 0, unroll=8)
    pltpu.make_async_copy(h_ref, xs_hbm.at[pl.ds(0, tm * SUBLANES)], sem).wait()


def _dispatch(zero_flag, dest, h_tiles, n_rows):
    t = h_tiles.shape[0] // SUBLANES
    tm = MOVE_TILE
    grid_spec = pltpu.PrefetchScalarGridSpec(
        num_scalar_prefetch=1,
        grid=(t // tm,),
        in_specs=[pl.BlockSpec((None, 1, tm), lambda i, zf: (i, 0, 0), memory_space=pltpu.SMEM),
                  pl.BlockSpec((tm * SUBLANES, LANES), lambda i, zf: (i, 0))],
        out_specs=pl.BlockSpec(memory_space=pl.ANY),
        scratch_shapes=[pltpu.VMEM((EXPERT_BLOCK * SUBLANES, LANES), U32),
                        pltpu.SemaphoreType.DMA(()), pltpu.SemaphoreType.DMA(())],
    )
    return pl.pallas_call(
        _dispatch_kernel,
        grid_spec=grid_spec,
        out_shape=jax.ShapeDtypeStruct((n_rows * SUBLANES, LANES), U32),
        compiler_params=_cparams(("arbitrary",)),
        name="moe_dispatch",
    )(zero_flag, dest.reshape(t // tm, 1, tm), h_tiles)


def _swiglu(x, w1_ref, wd_ref):
    f = wd_ref.shape[0]
    gu = jnp.dot(x, w1_ref[...], preferred_element_type=F32)
    gate, up = gu[:, :f], gu[:, f:]
    hidden = (gate * _sigmoid(gate) * up).astype(BF16)
    return jnp.dot(hidden, wd_ref[...], preferred_element_type=F32)


def _expert_kernel(ea_ref, eb_ref, used_ref, xs_ref, w1a_ref, wda_ref, w1b_ref, wdb_ref, y_ref):
    blk = xs_ref.shape[0] // SUBLANES
    d = w1a_ref.shape[0]
    n_word = d // 2 // LANES
    j = pl.program_id(0)

    @pl.when(j < used_ref[0])
    def _():
        words = [xs_ref[pl.ds(c, blk, stride=SUBLANES), :] for c in range(n_word)]
        lo = [pltpu.bitcast(lax.shift_left(w, jnp.uint32(16)), F32).astype(BF16) for w in words]
        hi = [pltpu.bitcast(w & jnp.uint32(0xFFFF0000), F32).astype(BF16) for w in words]
        x = jnp.concatenate(lo + hi, axis=-1)
        meta = pltpu.bitcast(xs_ref[pl.ds(n_word, blk, stride=SUBLANES), :], F32)
        wa, wb = meta[:, 0:1], meta[:, 1:2]
        y = wa * _swiglu(x, w1a_ref, wda_ref) + wb * _swiglu(x, w1b_ref, wdb_ref)
        for c in range(d // LANES):
            y_ref[pl.ds(c, blk, stride=SUBLANES), :] = y[:, c * LANES:(c + 1) * LANES]

    @pl.when(j >= used_ref[0])
    def _():
        y_ref[...] = jnp.zeros_like(y_ref)


def _experts(blk_ea, blk_eb, n_used, xs, w1, wd):
    n_e, d, f2 = w1.shape
    f = f2 // 2
    blk = EXPERT_BLOCK
    n_blk = xs.shape[0] // (blk * SUBLANES)
    assert d // LANES == SUBLANES
    grid_spec = pltpu.PrefetchScalarGridSpec(
        num_scalar_prefetch=3,
        grid=(n_blk,),
        in_specs=[pl.BlockSpec((blk * SUBLANES, LANES), lambda j, ea, eb, nu: (j, 0)),
                  pl.BlockSpec((None, d, f2), lambda j, ea, eb, nu: (ea[j], 0, 0)),
                  pl.BlockSpec((None, f, d), lambda j, ea, eb, nu: (ea[j], 0, 0)),
                  pl.BlockSpec((None, d, f2), lambda j, ea, eb, nu: (eb[j], 0, 0)),
                  pl.BlockSpec((None, f, d), lambda j, ea, eb, nu: (eb[j], 0, 0))],
        out_specs=pl.BlockSpec((blk * SUBLANES, LANES), lambda j, ea, eb, nu: (j, 0)),
    )
    return pl.pallas_call(
        _expert_kernel,
        grid_spec=grid_spec,
        out_shape=jax.ShapeDtypeStruct(xs.shape, F32),
        compiler_params=_cparams(("arbitrary",)),
        name="moe_experts",
    )(blk_ea, blk_eb, n_used, xs, w1, wd, w1, wd)


def _combine_kernel(dest_ref, y_hbm, x_ref, mod_ref, fg_ref, out_ref, y_scr, sem, *, final):
    tm, d = x_ref.shape

    def issue(r, c):
        _row_copy(y_hbm, dest_ref[0, r], y_scr, r, sem).start()
        return c

    lax.fori_loop(0, tm, issue, 0, unroll=8)
    pltpu.make_async_copy(y_hbm.at[pl.ds(0, tm * SUBLANES)], y_scr, sem).wait()
    cols = []
    for c in range(d // LANES):
        lanes = slice(c * LANES, (c + 1) * LANES)
        cols.append(x_ref[:, lanes] + mod_ref[5:6, lanes] * y_scr[pl.ds(c, tm, stride=SUBLANES), :])
    if final:
        ms = sum(jnp.sum(col * col, axis=-1, keepdims=True) for col in cols) * (1.0 / d)
        inv = lax.rsqrt(ms + NORM_EPS)
        cols = [col * inv * fg_ref[:, c * LANES:(c + 1) * LANES] for c, col in enumerate(cols)]
    for c, col in enumerate(cols):
        out_ref[:, c * LANES:(c + 1) * LANES] = col


def _combine(dest, y, x, mod_l, final_g, seq, final):
    t, d = x.shape
    tm = MOVE_TILE
    per_seq = seq // tm
    return pl.pallas_call(
        functools.partial(_combine_kernel, final=final),
        grid=(t // tm,),
        in_specs=[pl.BlockSpec((None, 1, tm), lambda i: (i, 0, 0), memory_space=pltpu.SMEM),
                  pl.BlockSpec(memory_space=pl.ANY),
                  pl.BlockSpec((tm, d), lambda i: (i, 0)),
                  pl.BlockSpec((None, 6, d), lambda i: (i // per_seq, 0, 0)),
                  pl.BlockSpec((1, d), lambda i: (0, 0))],
        out_specs=pl.BlockSpec((tm, d), lambda i: (i, 0)),
        out_shape=jax.ShapeDtypeStruct((t, d), F32),
        scratch_shapes=[pltpu.VMEM((tm * SUBLANES, LANES), F32), pltpu.SemaphoreType.DMA(())],
        compiler_params=_cparams(("arbitrary",)),
        name="moe_combine",
    )(dest.reshape(t // tm, 1, tm), y, x, mod_l, final_g.reshape(1, d))


def _moe_plan(meta_i, counts, n_e):
    n_cls = N_GROUPS * 6
    epg = n_e // N_GROUPS
    blk = EXPERT_BLOCK
    t = meta_i.shape[1]
    n_blk = t // blk + n_cls
    cls, rank = meta_i[0], meta_i[1]
    cnt = counts[:n_cls, 0].astype(I32)
    padded = (cnt + blk - 1) // blk * blk
    pad_end = jnp.cumsum(padded)
    pad_start = pad_end - padded
    dest = jnp.sum(jnp.where(cls[None, :] == jnp.arange(n_cls, dtype=I32)[:, None], pad_start[:, None], 0),
                   axis=0).astype(I32) + rank
    n_used = (pad_end[-1] // blk).astype(I32)
    blk_start = jnp.arange(n_blk, dtype=I32) * blk
    blk_cls = jnp.minimum(jnp.sum(blk_start[:, None] >= pad_end[None, :], axis=1), n_cls - 1).astype(I32)
    pair_lo = jnp.array([0, 0, 0, 1, 1, 2], I32)
    pair_hi = jnp.array([1, 2, 3, 2, 3, 3], I32)
    blk_ea = (blk_cls // 6) * epg + pair_lo[blk_cls % 6]
    blk_eb = (blk_cls // 6) * epg + pair_hi[blk_cls % 6]
    last_blk = pad_end // blk - 1
    partial = jnp.logical_and(cnt % blk != 0, cnt > 0)
    is_last_partial = jnp.any(jnp.logical_and(jnp.arange(n_blk, dtype=I32)[:, None] == last_blk[None, :],
                                              partial[None, :]), axis=1)
    zero_flag = jnp.logical_or(is_last_partial, jnp.arange(n_blk, dtype=I32) >= n_used).astype(I32)
    return dest, blk_ea.astype(I32), blk_eb.astype(I32), n_used.reshape(1), zero_flag, n_blk * blk


def _moe(x, g, mod_l, w_r_t, b_r, w1, wd, final_g, seq, final):
    n_e = w_r_t.shape[0]
    h_ext, meta_i, counts = _router(x, g, mod_l, w_r_t, b_r, seq)
    dest, blk_ea, blk_eb, n_used, zero_flag, n_rows = _moe_plan(meta_i, counts, n_e)
    xs = _dispatch(zero_flag, dest, h_ext, n_rows)
    y = _experts(blk_ea, blk_eb, n_used, xs, w1, wd)
    return _combine(dest, y, x, mod_l, final_g, seq, final)


def kernel(x, c, norm1_g, norm2_g, w_ada, b_ada, sb_w_qkv, sb_w_o, cv_w_in, cv_b_in, cv_w_dw, cv_b_dw, cv_ln_g, cv_ln_b, cv_w_out, cv_b_out, w_router, b_router, moe_w_gate, moe_w_up, moe_w_down, final_g):
    batch, seq, d = x.shape
    depth = w_ada.shape[0]
    assert d == N_HEADS * (LANES // 2) and seq % ROW_TILE == 0 and seq % ATT_BLOCK == 0
    assert (batch * seq) % EXPERT_BLOCK == 0 and w_router.shape[1] == 4 * N_GROUPS

    mod = _ada(c, w_ada, b_ada).reshape(depth, batch, 6, d)
    w_r_t = w_router.T.astype(BF16)
    xf = x.reshape(batch * seq, d)
    for layer in range(depth):
        mod_l = mod[layer]
        j = layer // 2
        if layer % 2 == 0:
            qkv = _qkv_proj(xf, norm1_g[layer], mod_l, sb_w_qkv[j].astype(BF16), seq)
            o = _sb_attention(qkv, batch, seq, d)
            xf = _out_proj(o, sb_w_o[j].astype(BF16), xf, mod_l, seq)
        else:
            u = _cv_in(xf, norm1_g[layer], mod_l, cv_w_in[j].astype(BF16), cv_b_in[j], seq)
            xf = _cv_out(u, cv_w_dw[j], cv_b_dw[j], cv_ln_g[j], cv_ln_b[j],
                         cv_w_out[j].astype(BF16), cv_b_out[j], xf, mod_l, seq)
        w1 = jnp.concatenate([moe_w_gate[layer], moe_w_up[layer]], axis=-1).astype(BF16)
        wd = moe_w_down[layer].astype(BF16)
        xf = _moe(xf, norm2_g[layer], mod_l, w_r_t, b_router, w1, wd, final_g, seq, layer == depth - 1)
    return xf.reshape(batch, seq, d)
```

```python
import functools

import jax
import jax.numpy as jnp
from jax import lax
from jax.experimental import pallas as pl
from jax.experimental.pallas import tpu as pltpu

F32 = jnp.float32
BF16 = jnp.bfloat16
I32 = jnp.int32
U32 = jnp.uint32

N_HEADS = 16
N_GROUPS = 4
NORM_EPS = 1e-6
LOG2_E = 1.4426950408889634
MASKED_LOG = -1e30

LANES = 128
SUBLANES = 8
MXU_DIM = 256
VMEM_LIMIT = 56 * 1024 * 1024

ROW_TILE = 512
ATT_BLOCK = MXU_DIM
ATT_PAIRS = 4
CONV_TILE = 256
CONV_HALO = 32
CONV_CHUNK = 128
EXPERT_BLOCK = 256
ROUTE_TILE = 512
MOVE_TILE = 1024
N_CLASS_PAD = 32


def _cparams(sem):
    return pltpu.CompilerParams(dimension_semantics=sem, vmem_limit_bytes=VMEM_LIMIT)


def _sigmoid(x):
    return 1.0 / (1.0 + jnp.exp(-x))


def _norm_mod(x, g, shift, scale):
    ms = jnp.mean(x * x, axis=-1, keepdims=True)
    y = x * lax.rsqrt(ms + NORM_EPS) * g
    return y * (1.0 + scale) + shift


def _ada_kernel(c_ref, w_ref, b_ref, o_ref):
    c = c_ref[...]
    cond = (c * _sigmoid(c)).astype(BF16)
    o_ref[...] = jnp.dot(cond, w_ref[...].astype(BF16), preferred_element_type=F32) + b_ref[...]


def _ada(c, w_ada, b_ada):
    depth, d, n = w_ada.shape
    b = c.shape[0]
    tn = 1536
    return pl.pallas_call(
        _ada_kernel,
        grid=(depth, n // tn),
        in_specs=[pl.BlockSpec((b, d), lambda l, j: (0, 0)),
                  pl.BlockSpec((None, d, tn), lambda l, j: (l, 0, j)),
                  pl.BlockSpec((None, 1, tn), lambda l, j: (l, 0, j))],
        out_specs=pl.BlockSpec((None, b, tn), lambda l, j: (l, 0, j)),
        out_shape=jax.ShapeDtypeStruct((depth, b, n), F32),
        compiler_params=_cparams(("parallel", "parallel")),
        name="ada",
    )(c, w_ada, b_ada.reshape(depth, 1, n))


def _qkv_kernel(x_ref, g_ref, mod_ref, w_ref, o_ref):
    h = _norm_mod(x_ref[...], g_ref[...], mod_ref[0:1, :], mod_ref[1:2, :]).astype(BF16)
    n = o_ref.shape[1]
    tn = 1024
    for j in range(n // tn):
        o_ref[:, j * tn:(j + 1) * tn] = jnp.dot(
            h, w_ref[:, j * tn:(j + 1) * tn], preferred_element_type=F32).astype(o_ref.dtype)


def _qkv_proj(x, g, mod_l, w, seq):
    t, d = x.shape
    n = w.shape[1]
    tm = ROW_TILE
    per_seq = seq // tm
    return pl.pallas_call(
        _qkv_kernel,
        grid=(t // tm,),
        in_specs=[pl.BlockSpec((tm, d), lambda i: (i, 0)),
                  pl.BlockSpec((1, d), lambda i: (0, 0)),
                  pl.BlockSpec((None, 6, d), lambda i: (i // per_seq, 0, 0)),
                  pl.BlockSpec((d, n), lambda i: (0, 0))],
        out_specs=pl.BlockSpec((tm, n), lambda i: (i, 0)),
        out_shape=jax.ShapeDtypeStruct((t, n), BF16),
        compiler_params=_cparams(("parallel",)),
        name="sb_qkv",
    )(x, g.reshape(1, d), mod_l, w)


def _oproj_kernel(o_ref, w_ref, x_ref, mod_ref, out_ref):
    y = jnp.dot(o_ref[...], w_ref[...], preferred_element_type=F32)
    out_ref[...] = x_ref[...] + mod_ref[2:3, :] * y


def _out_proj(o, w, x, mod_l, seq):
    t, d = x.shape
    tm = ROW_TILE
    per_seq = seq // tm
    return pl.pallas_call(
        _oproj_kernel,
        grid=(t // tm,),
        in_specs=[pl.BlockSpec((tm, d), lambda i: (i, 0)),
                  pl.BlockSpec((d, d), lambda i: (0, 0)),
                  pl.BlockSpec((tm, d), lambda i: (i, 0)),
                  pl.BlockSpec((None, 6, d), lambda i: (i // per_seq, 0, 0))],
        out_specs=pl.BlockSpec((tm, d), lambda i: (i, 0)),
        out_shape=jax.ShapeDtypeStruct((t, d), F32),
        compiler_params=_cparams(("parallel",)),
        name="sb_out",
    )(o, w, x, mod_l)


def _sb_attn_kernel(q_ref, k_ref, v_ref, o_ref, r_scr, o_scr):
    seq = q_ref.shape[0]
    tb = ATT_BLOCK
    half = LANES // 2
    head0 = lax.broadcasted_iota(I32, (1, LANES), 1) < half
    row = lax.broadcasted_iota(I32, (tb, tb), 0)
    col = lax.broadcasted_iota(I32, (tb, tb), 1)
    causal = col < row
    causal2 = jnp.concatenate([causal, causal], axis=0)
    upper = (row > col).astype(BF16)
    sign = jnp.uint32(0x80000000)
    n_q = seq // tb
    pairs = range(ATT_PAIRS)

    def lanes(p):
        return slice(p * LANES, (p + 1) * LANES)

    def q_block(qi, carry):
        q0 = pl.multiple_of(qi * tb, tb)
        qq = []
        for p in pairs:
            q2 = q_ref[pl.ds(q0, tb), lanes(p)] * 0.125
            zero = jnp.zeros_like(q2)
            qq.append(jnp.concatenate([jnp.where(head0, q2, zero), jnp.where(head0, zero, q2)], axis=0))
        r_scr[...] = jnp.zeros_like(r_scr)
        o_scr[...] = jnp.zeros_like(o_scr)

        def kv_block(kb, mask):
            k0 = pl.multiple_of(kb * tb, tb)
            zs = [lax.dot_general(qq[p], k_ref[pl.ds(k0, tb), lanes(p)], (((1,), (1,)), ((), ())),
                                  preferred_element_type=F32) * LOG2_E for p in pairs]
            log_beta, log_1m = [], []
            for p in pairs:
                z = zs[p]
                neg_abs = pltpu.bitcast(pltpu.bitcast(z, U32) | sign, F32)
                lb = jnp.minimum(z, 0.0) - jnp.log2(1.0 + jnp.exp2(neg_abs))
                l1 = lb - z
                if mask is not None:
                    l1 = jnp.where(mask, l1, 0.0)
                    lb = jnp.where(mask, lb, MASKED_LOG)
                log_beta.append(lb)
                log_1m.append(l1.astype(BF16))
            inner = [jnp.dot(log_1m[p], upper, preferred_element_type=F32) for p in pairs]
            probs = [jnp.exp2(log_beta[p] + inner[p]).astype(BF16) for p in pairs]
            for p in pairs:
                r = r_scr[p]
                pv = jnp.dot(probs[p], v_ref[pl.ds(k0, tb), lanes(p)], preferred_element_type=F32)
                o_scr[p] += jnp.exp2(r) * pv
                r_scr[p] = r + inner[p][:, 0:1] + log_1m[p][:, 0:1].astype(F32)

        kv_block(qi, causal2)

        def below(i, c):
            kv_block(qi - 1 - i, None)
            return c

        lax.fori_loop(0, qi, below, 0)
        for p in pairs:
            o_ref[pl.ds(q0, tb), lanes(p)] = jnp.where(head0, o_scr[p, :tb], o_scr[p, tb:]).astype(o_ref.dtype)
        return carry

    lax.fori_loop(0, n_q, q_block, 0)


def _sb_attention(qkv, batch, seq, d):
    t = batch * seq
    width = ATT_PAIRS * LANES
    groups = d // width
    tb = ATT_BLOCK
    return pl.pallas_call(
        _sb_attn_kernel,
        grid=(batch, groups),
        in_specs=[pl.BlockSpec((seq, width), lambda b, p: (b, p)),
                  pl.BlockSpec((seq, width), lambda b, p: (b, groups + p)),
                  pl.BlockSpec((seq, width), lambda b, p: (b, 2 * groups + p))],
        out_specs=pl.BlockSpec((seq, width), lambda b, p: (b, p)),
        out_shape=jax.ShapeDtypeStruct((t, d), BF16),
        scratch_shapes=[pltpu.VMEM((ATT_PAIRS, 2 * tb, 1), F32),
                        pltpu.VMEM((ATT_PAIRS, 2 * tb, LANES), F32)],
        compiler_params=_cparams(("parallel", "parallel")),
        name="sb_attn",
    )(qkv, qkv, qkv)


def _cv_in_kernel(x_ref, g_ref, mod_ref, w_ref, b_ref, o_ref):
    d = x_ref.shape[1]
    h = _norm_mod(x_ref[...], g_ref[...], mod_ref[0:1, :], mod_ref[1:2, :]).astype(BF16)
    a = jnp.dot(h, w_ref[:, :d], preferred_element_type=F32) + b_ref[:, :d]
    gate = jnp.dot(h, w_ref[:, d:], preferred_element_type=F32) + b_ref[:, d:]
    o_ref[...] = a * _sigmoid(gate)


def _cv_in(x, g, mod_l, w, b, seq):
    t, d = x.shape
    tm = ROW_TILE
    per_seq = seq // tm
    return pl.pallas_call(
        _cv_in_kernel,
        grid=(t // tm,),
        in_specs=[pl.BlockSpec((tm, d), lambda i: (i, 0)),
                  pl.BlockSpec((1, d), lambda i: (0, 0)),
                  pl.BlockSpec((None, 6, d), lambda i: (i // per_seq, 0, 0)),
                  pl.BlockSpec((d, 2 * d), lambda i: (0, 0)),
                  pl.BlockSpec((1, 2 * d), lambda i: (0, 0))],
        out_specs=pl.BlockSpec((tm, d), lambda i: (i, 0)),
        out_shape=jax.ShapeDtypeStruct((t, d), F32),
        compiler_params=_cparams(("parallel",)),
        name="cv_in",
    )(x, g.reshape(1, d), mod_l, w, b.reshape(1, 2 * d))


def _cv_out_kernel(u_ref, halo_ref, wdw_ref, bdw_ref, lng_ref, lnb_ref, w_ref, b_ref, x_ref, mod_ref,
                   out_ref, ext_scr, conv_scr, *, per_seq):
    tm, d = u_ref.shape
    width = wdw_ref.shape[0]
    first = (pl.program_id(0) % per_seq) == 0
    halo = halo_ref[...]
    ext_scr[0:CONV_HALO, :] = jnp.where(first, jnp.zeros_like(halo), halo)
    ext_scr[CONV_HALO:, :] = u_ref[...]
    lead = CONV_HALO - (width - 1)
    rows = CONV_CHUNK
    win = rows + CONV_HALO
    for rc in range(tm // rows):
        for c in range(d // LANES):
            lanes = slice(c * LANES, (c + 1) * LANES)
            window = ext_scr[rc * rows:rc * rows + win, lanes]
            acc = jnp.zeros((rows, LANES), F32)
            for s in range(SUBLANES):
                taps = [k for k in range(width) if (lead + k) % SUBLANES == s]
                if not taps:
                    continue
                shifted = window if s == 0 else pltpu.roll(window, win - s, axis=0)
                for k in taps:
                    m = (lead + k) // SUBLANES * SUBLANES
                    acc = acc + shifted[m:m + rows, :] * wdw_ref[k:k + 1, lanes]
            conv_scr[rc * rows:(rc + 1) * rows, lanes] = acc
    v = conv_scr[...] + bdw_ref[...]
    mu = jnp.mean(v, axis=-1, keepdims=True)
    var = jnp.mean(jnp.square(v - mu), axis=-1, keepdims=True)
    y = (v - mu) * lax.rsqrt(var + NORM_EPS) * lng_ref[...] + lnb_ref[...]
    y = (y * _sigmoid(y)).astype(BF16)
    z = jnp.dot(y, w_ref[...], preferred_element_type=F32) + b_ref[...]
    out_ref[...] = x_ref[...] + mod_ref[2:3, :] * z


def _cv_out(u, w_dw, b_dw, ln_g, ln_b, w, b, x, mod_l, seq):
    t, d = x.shape
    tm = CONV_TILE
    per_seq = seq // tm
    width = w_dw.shape[0]
    assert width - 1 <= CONV_HALO and tm % CONV_HALO == 0
    halo_per_tile = tm // CONV_HALO
    row = lambda i: (0, 0)
    return pl.pallas_call(
        functools.partial(_cv_out_kernel, per_seq=per_seq),
        grid=(t // tm,),
        in_specs=[pl.BlockSpec((tm, d), lambda i: (i, 0)),
                  pl.BlockSpec((CONV_HALO, d), lambda i: (jnp.maximum(i * halo_per_tile - 1, 0), 0)),
                  pl.BlockSpec((width, d), row),
                  pl.BlockSpec((1, d), row), pl.BlockSpec((1, d), row), pl.BlockSpec((1, d), row),
                  pl.BlockSpec((d, d), row), pl.BlockSpec((1, d), row),
                  pl.BlockSpec((tm, d), lambda i: (i, 0)),
                  pl.BlockSpec((None, 6, d), lambda i: (i // per_seq, 0, 0))],
        out_specs=pl.BlockSpec((tm, d), lambda i: (i, 0)),
        out_shape=jax.ShapeDtypeStruct((t, d), F32),
        scratch_shapes=[pltpu.VMEM((tm + CONV_HALO, d), F32), pltpu.VMEM((tm, d), F32)],
        compiler_params=_cparams(("parallel",)),
        name="cv_out",
    )(u, u, w_dw, b_dw.reshape(1, d), ln_g.reshape(1, d), ln_b.reshape(1, d), w, b.reshape(1, d), x, mod_l)


def _top2_of4(v):
    m0, i0 = v[0], jnp.zeros(v[0].shape, I32)
    for j in range(1, 4):
        gt = v[j] > m0
        m0 = jnp.where(gt, v[j], m0)
        i0 = jnp.where(gt, j, i0)
    m1 = jnp.full(v[0].shape, -jnp.inf, F32)
    i1 = jnp.zeros(v[0].shape, I32)
    for j in range(4):
        gt = jnp.logical_and(i0 != j, v[j] > m1)
        m1 = jnp.where(gt, v[j], m1)
        i1 = jnp.where(gt, j, i1)
    return i0, i1


def _pick4(i, v):
    return jnp.where(i == 0, v[0], jnp.where(i == 1, v[1], jnp.where(i == 2, v[2], v[3])))


def _router_kernel(x_ref, g_ref, mod_ref, wr_ref, br_ref, h_ref, mi_ref, cnt_ref, base_scr):
    tm, d = x_ref.shape
    n_e = wr_ref.shape[0]
    epg = n_e // N_GROUPS

    @pl.when(pl.program_id(0) == 0)
    def _():
        base_scr[...] = jnp.zeros_like(base_scr)

    hb = _norm_mod(x_ref[...], g_ref[...], mod_ref[3:4, :], mod_ref[4:5, :]).astype(BF16)
    bits = pltpu.bitcast(hb.astype(F32), U32)
    half = d // 2
    words = (bits[:, half:] & jnp.uint32(0xFFFF0000)) | lax.shift_right_logical(bits[:, :half], jnp.uint32(16))
    for c in range(half // LANES):
        h_ref[pl.ds(c, tm, stride=SUBLANES), :] = words[:, c * LANES:(c + 1) * LANES]
    for c in range(half // LANES + 1, SUBLANES):
        h_ref[pl.ds(c, tm, stride=SUBLANES), :] = jnp.zeros((tm, LANES), U32)
    logits = lax.dot_general(wr_ref[...], hb, (((1,), (1,)), ((), ())),
                             preferred_element_type=F32)
    scores = _sigmoid(logits)
    sel = scores + br_ref[...]
    sel_r = [sel[e:e + 1, :] for e in range(n_e)]
    sc_r = [scores[e:e + 1, :] for e in range(n_e)]

    best, gi = None, None
    for g in range(N_GROUPS):
        a, b, c, e4 = sel_r[g * epg:(g + 1) * epg]
        hi1, lo1 = jnp.maximum(a, b), jnp.minimum(a, b)
        hi2, lo2 = jnp.maximum(c, e4), jnp.minimum(c, e4)
        top1 = jnp.maximum(hi1, hi2)
        top2 = jnp.maximum(jnp.minimum(hi1, hi2), jnp.maximum(lo1, lo2))
        gs = top1 + top2
        if g == 0:
            best, gi = gs, jnp.zeros(gs.shape, I32)
        else:
            gt = gs > best
            best = jnp.where(gt, gs, best)
            gi = jnp.where(gt, g, gi)

    in_sel = [_pick4(gi, [sel_r[g * epg + j] for g in range(N_GROUPS)]) for j in range(epg)]
    in_sc = [_pick4(gi, [sc_r[g * epg + j] for g in range(N_GROUPS)]) for j in range(epg)]
    i0, i1 = _top2_of4(in_sel)
    w0, w1 = _pick4(i0, in_sc), _pick4(i1, in_sc)
    wsum = w0 + w1
    w0, w1 = w0 / wsum, w1 / wsum

    lo, hi = jnp.minimum(i0, i1), jnp.maximum(i0, i1)
    pair = jnp.where(lo == 0, 0, jnp.where(lo == 1, 3, 5)) + hi - lo - 1
    cls = gi * 6 + pair
    wa = jnp.where(i0 < i1, w0, w1)
    wb = jnp.where(i0 < i1, w1, w0)

    onehot = (lax.broadcasted_iota(I32, (N_CLASS_PAD, tm), 0) == cls).astype(BF16)
    before = (lax.broadcasted_iota(I32, (tm, tm), 0) < lax.broadcasted_iota(I32, (tm, tm), 1)).astype(BF16)
    prefix = jnp.dot(onehot, before, preferred_element_type=F32)
    base = base_scr[...]
    oh = onehot.astype(F32)
    rank = jnp.sum(oh * (prefix + base), axis=0, keepdims=True)
    base = base + jnp.sum(oh, axis=1, keepdims=True)
    base_scr[...] = base
    cnt_ref[...] = jnp.broadcast_to(base, cnt_ref.shape)

    mi_ref[...] = jnp.concatenate(
        [cls, rank.astype(I32), jnp.zeros((SUBLANES - 2, tm), I32)], axis=0)
    wrows = jnp.concatenate([wa, wb, jnp.zeros((LANES - 2, tm), F32)], axis=0)
    h_ref[pl.ds(half // LANES, tm, stride=SUBLANES), :] = pltpu.bitcast(wrows.T, U32)


def _router(x, g, mod_l, w_r_t, b_r, seq):
    t, d = x.shape
    n_e = w_r_t.shape[0]
    tm = ROUTE_TILE
    per_seq = seq // tm
    return pl.pallas_call(
        _router_kernel,
        grid=(t // tm,),
        in_specs=[pl.BlockSpec((tm, d), lambda i: (i, 0)),
                  pl.BlockSpec((1, d), lambda i: (0, 0)),
                  pl.BlockSpec((None, 6, d), lambda i: (i // per_seq, 0, 0)),
                  pl.BlockSpec((n_e, d), lambda i: (0, 0)),
                  pl.BlockSpec((n_e, 1), lambda i: (0, 0))],
        out_specs=[pl.BlockSpec((tm * SUBLANES, LANES), lambda i: (i, 0)),
                   pl.BlockSpec((SUBLANES, tm), lambda i: (0, i)),
                   pl.BlockSpec((N_CLASS_PAD, LANES), lambda i: (0, 0))],
        out_shape=[jax.ShapeDtypeStruct((t * SUBLANES, LANES), U32),
                   jax.ShapeDtypeStruct((SUBLANES, t), I32),
                   jax.ShapeDtypeStruct((N_CLASS_PAD, LANES), F32)],
        scratch_shapes=[pltpu.VMEM((N_CLASS_PAD, 1), F32)],
        compiler_params=_cparams(("arbitrary",)),
        name="moe_router",
    )(x, g.reshape(1, d), mod_l, w_r_t, b_r.reshape(n_e, 1))


def _row_copy(src, s, dst, r, sem):
    return pltpu.make_async_copy(src.at[pl.ds(pl.multiple_of(s * SUBLANES, SUBLANES), SUBLANES)],
                                 dst.at[pl.ds(pl.multiple_of(r * SUBLANES, SUBLANES), SUBLANES)], sem)


def _dispatch_kernel(zero_flag_ref, dest_ref, h_ref, xs_hbm, zero_scr, sem, zsem):
    tm = dest_ref.shape[1]
    blk = zero_scr.shape[0]
    n_blk = xs_hbm.shape[0] // blk
    i = pl.program_id(0)

    @pl.when(i == 0)
    def _():
        zero_scr[...] = jnp.zeros_like(zero_scr)

        def fill(j, n):
            flag = zero_flag_ref[j]

            @pl.when(flag == 1)
            def _():
                pltpu.make_async_copy(zero_scr, xs_hbm.at[pl.ds(pl.multiple_of(j * blk, blk), blk)], zsem).start()

            return n + flag

        n_fill = lax.fori_loop(0, n_blk, fill, 0)

        def drain(j, c):
            pltpu.make_async_copy(zero_scr, xs_hbm.at[pl.ds(0, blk)], zsem).wait()
            return c

        lax.fori_loop(0, n_fill, drain, 0)

    _issue_rows(lambda r: _row_copy(h_ref, r, xs_hbm, dest_ref[0, r], sem), tm)
    pltpu.make_async_copy(h_ref, xs_hbm.at[pl.ds(0, tm * SUBLANES)], sem).wait()


def _dispatch(zero_flag, dest, h_tiles, n_rows):
    t = h_tiles.shape[0] // SUBLANES
    tm = MOVE_TILE
    grid_spec = pltpu.PrefetchScalarGridSpec(
        num_scalar_prefetch=1,
        grid=(t // tm,),
        in_specs=[pl.BlockSpec((None, 1, tm), lambda i, zf: (i, 0, 0), memory_space=pltpu.SMEM),
                  pl.BlockSpec((tm * SUBLANES, LANES), lambda i, zf: (i, 0))],
        out_specs=pl.BlockSpec(memory_space=pl.ANY),
        scratch_shapes=[pltpu.VMEM((EXPERT_BLOCK * SUBLANES, LANES), U32),
                        pltpu.SemaphoreType.DMA(()), pltpu.SemaphoreType.DMA(())],
    )
    return pl.pallas_call(
        _dispatch_kernel,
        grid_spec=grid_spec,
        out_shape=jax.ShapeDtypeStruct((n_rows * SUBLANES, LANES), U32),
        compiler_params=_cparams(("arbitrary",)),
        name="moe_dispatch",
    )(zero_flag, dest.reshape(t // tm, 1, tm), h_tiles)


def _swiglu(x, w1_ref, wd_ref):
    f = wd_ref.shape[0]
    gu = jnp.dot(x, w1_ref[...], preferred_element_type=F32)
    gate, up = gu[:, :f], gu[:, f:]
    hidden = (gate * _sigmoid(gate) * up).astype(BF16)
    return jnp.dot(hidden, wd_ref[...], preferred_element_type=F32)


def _expert_kernel(ea_ref, eb_ref, used_ref, xs_ref, w1a_ref, wda_ref, w1b_ref, wdb_ref, y_ref):
    blk = xs_ref.shape[0] // SUBLANES
    d = w1a_ref.shape[0]
    n_word = d // 2 // LANES
    j = pl.program_id(0)

    @pl.when(j < used_ref[0])
    def _():
        words = [xs_ref[pl.ds(c, blk, stride=SUBLANES), :] for c in range(n_word)]
        lo = [pltpu.bitcast(lax.shift_left(w, jnp.uint32(16)), F32).astype(BF16) for w in words]
        hi = [pltpu.bitcast(w & jnp.uint32(0xFFFF0000), F32).astype(BF16) for w in words]
        x = jnp.concatenate(lo + hi, axis=-1)
        meta = pltpu.bitcast(xs_ref[pl.ds(n_word, blk, stride=SUBLANES), :], F32)
        wa, wb = meta[:, 0:1], meta[:, 1:2]
        y = wa * _swiglu(x, w1a_ref, wda_ref) + wb * _swiglu(x, w1b_ref, wdb_ref)
        for c in range(d // LANES):
            y_ref[pl.ds(c, blk, stride=SUBLANES), :] = y[:, c * LANES:(c + 1) * LANES]

    @pl.when(j >= used_ref[0])
    def _():
        y_ref[...] = jnp.zeros_like(y_ref)


def _experts(blk_ea, blk_eb, n_used, xs, w1, wd):
    n_e, d, f2 = w1.shape
    f = f2 // 2
    blk = EXPERT_BLOCK
    n_blk = xs.shape[0] // (blk * SUBLANES)
    assert d // LANES == SUBLANES
    grid_spec = pltpu.PrefetchScalarGridSpec(
        num_scalar_prefetch=3,
        grid=(n_blk,),
        in_specs=[pl.BlockSpec((blk * SUBLANES, LANES), lambda j, ea, eb, nu: (j, 0)),
                  pl.BlockSpec((None, d, f2), lambda j, ea, eb, nu: (ea[j], 0, 0)),
                  pl.BlockSpec((None, f, d), lambda j, ea, eb, nu: (ea[j], 0, 0)),
                  pl.BlockSpec((None, d, f2), lambda j, ea, eb, nu: (eb[j], 0, 0)),
                  pl.BlockSpec((None, f, d), lambda j, ea, eb, nu: (eb[j], 0, 0))],
        out_specs=pl.BlockSpec((blk * SUBLANES, LANES), lambda j, ea, eb, nu: (j, 0)),
    )
    return pl.pallas_call(
        _expert_kernel,
        grid_spec=grid_spec,
        out_shape=jax.ShapeDtypeStruct(xs.shape, F32),
        compiler_params=_cparams(("arbitrary",)),
        name="moe_experts",
    )(blk_ea, blk_eb, n_used, xs, w1, wd, w1, wd)


def _issue_rows(copy_of_row, n_rows):
    def group(g, c):
        for k in range(SUBLANES):
            copy_of_row(g * SUBLANES + k).start(priority=k % 2)
        return c

    lax.fori_loop(0, n_rows // SUBLANES, group, 0)


def _moe_update(dest_ref, dest_next_ref, y_hbm, x_ref, mod_ref, y_scr, sem):
    tm, d = x_ref.shape
    i = pl.program_id(0)

    def request(idx_ref, slot):
        _issue_rows(lambda r: _row_copy(y_hbm, idx_ref[0, r], y_scr.at[slot], r, sem.at[slot]), tm)

    @pl.when(i == 0)
    def _():
        request(dest_ref, 0)

    @pl.when(i + 1 < pl.num_programs(0))
    def _():
        request(dest_next_ref, (i + 1) % 2)

    slot = i % 2
    pltpu.make_async_copy(y_hbm.at[pl.ds(0, tm * SUBLANES)], y_scr.at[slot], sem.at[slot]).wait()
    cols = [x_ref[:, c * LANES:(c + 1) * LANES]
            + mod_ref[5:6, c * LANES:(c + 1) * LANES] * y_scr[slot, pl.ds(c, tm, stride=SUBLANES), :]
            for c in range(d // LANES)]
    return jnp.concatenate(cols, axis=-1)


def _combine_final_kernel(dest_ref, dest_next_ref, y_hbm, x_ref, mod_ref, fg_ref, out_ref, y_scr, sem):
    x = _moe_update(dest_ref, dest_next_ref, y_hbm, x_ref, mod_ref, y_scr, sem)
    ms = jnp.mean(x * x, axis=-1, keepdims=True)
    out_ref[...] = x * lax.rsqrt(ms + NORM_EPS) * fg_ref[...]


def _combine_qkv_kernel(dest_ref, dest_next_ref, y_hbm, x_ref, mod_ref, g_ref, modn_ref, w_ref,
                        x_out_ref, qkv_ref, y_scr, sem):
    x = _moe_update(dest_ref, dest_next_ref, y_hbm, x_ref, mod_ref, y_scr, sem)
    x_out_ref[...] = x
    h = _norm_mod(x, g_ref[...], modn_ref[0:1, :], modn_ref[1:2, :]).astype(BF16)
    tn = x_ref.shape[1]
    for j in range(qkv_ref.shape[1] // tn):
        qkv_ref[:, j * tn:(j + 1) * tn] = jnp.dot(
            h, w_ref[:, j * tn:(j + 1) * tn], preferred_element_type=F32).astype(qkv_ref.dtype)


def _combine_cvin_kernel(dest_ref, dest_next_ref, y_hbm, x_ref, mod_ref, g_ref, modn_ref, w_ref, b_ref,
                         x_out_ref, u_ref, y_scr, sem):
    d = x_ref.shape[1]
    x = _moe_update(dest_ref, dest_next_ref, y_hbm, x_ref, mod_ref, y_scr, sem)
    x_out_ref[...] = x
    h = _norm_mod(x, g_ref[...], modn_ref[0:1, :], modn_ref[1:2, :]).astype(BF16)
    a = jnp.dot(h, w_ref[:, :d], preferred_element_type=F32) + b_ref[:, :d]
    gate = jnp.dot(h, w_ref[:, d:], preferred_element_type=F32) + b_ref[:, d:]
    u_ref[...] = a * _sigmoid(gate)


def _combine(dest, y, x, mod_l, seq, tail, tail_args):
    t, d = x.shape
    tm = ROW_TILE
    per_seq = seq // tm
    n_tiles = t // tm
    row = lambda i: (0, 0)
    tile = pl.BlockSpec((tm, d), lambda i: (i, 0))
    mod_spec = pl.BlockSpec((None, 6, d), lambda i: (i // per_seq, 0, 0))
    in_specs = [pl.BlockSpec((None, 1, tm), lambda i: (i, 0, 0), memory_space=pltpu.SMEM),
                pl.BlockSpec((None, 1, tm), lambda i: (jnp.minimum(i + 1, n_tiles - 1), 0, 0),
                             memory_space=pltpu.SMEM),
                pl.BlockSpec(memory_space=pl.ANY), tile, mod_spec]
    dest3 = dest.reshape(n_tiles, 1, tm)
    args = [dest3, dest3, y, x, mod_l]
    x_shape = jax.ShapeDtypeStruct((t, d), F32)
    if tail == "final":
        (final_g,) = tail_args
        body, out_specs, out_shape = _combine_final_kernel, tile, x_shape
        in_specs += [pl.BlockSpec((1, d), row)]
        args += [final_g.reshape(1, d)]
    elif tail == "qkv":
        g, mod_n, w = tail_args
        n = w.shape[1]
        body = _combine_qkv_kernel
        out_specs = [tile, pl.BlockSpec((tm, n), lambda i: (i, 0))]
        out_shape = [x_shape, jax.ShapeDtypeStruct((t, n), BF16)]
        in_specs += [pl.BlockSpec((1, d), row), mod_spec, pl.BlockSpec((d, n), row)]
        args += [g.reshape(1, d), mod_n, w]
    else:
        g, mod_n, w, b = tail_args
        body = _combine_cvin_kernel
        out_specs = [tile, tile]
        out_shape = [x_shape, x_shape]
        in_specs += [pl.BlockSpec((1, d), row), mod_spec, pl.BlockSpec((d, 2 * d), row),
                     pl.BlockSpec((1, 2 * d), row)]
        args += [g.reshape(1, d), mod_n, w, b.reshape(1, 2 * d)]
    return pl.pallas_call(
        body,
        grid=(n_tiles,),
        in_specs=in_specs,
        out_specs=out_specs,
        out_shape=out_shape,
        scratch_shapes=[pltpu.VMEM((2, tm * SUBLANES, LANES), F32), pltpu.SemaphoreType.DMA((2,))],
        compiler_params=_cparams(("arbitrary",)),
        name="moe_combine_" + tail,
    )(*args)


def _moe_plan(meta_i, counts, n_e):
    n_cls = N_GROUPS * 6
    epg = n_e // N_GROUPS
    blk = EXPERT_BLOCK
    t = meta_i.shape[1]
    n_blk = t // blk + n_cls
    cls, rank = meta_i[0], meta_i[1]
    cnt = counts[:n_cls, 0].astype(I32)
    padded = (cnt + blk - 1) // blk * blk
    pad_end = jnp.cumsum(padded)
    pad_start = pad_end - padded
    dest = jnp.sum(jnp.where(cls[None, :] == jnp.arange(n_cls, dtype=I32)[:, None], pad_start[:, None], 0),
                   axis=0).astype(I32) + rank
    n_used = (pad_end[-1] // blk).astype(I32)
    blk_start = jnp.arange(n_blk, dtype=I32) * blk
    blk_cls = jnp.minimum(jnp.sum(blk_start[:, None] >= pad_end[None, :], axis=1), n_cls - 1).astype(I32)
    pair_lo = jnp.array([0, 0, 0, 1, 1, 2], I32)
    pair_hi = jnp.array([1, 2, 3, 2, 3, 3], I32)
    blk_ea = (blk_cls // 6) * epg + pair_lo[blk_cls % 6]
    blk_eb = (blk_cls // 6) * epg + pair_hi[blk_cls % 6]
    last_blk = pad_end // blk - 1
    partial = jnp.logical_and(cnt % blk != 0, cnt > 0)
    is_last_partial = jnp.any(jnp.logical_and(jnp.arange(n_blk, dtype=I32)[:, None] == last_blk[None, :],
                                              partial[None, :]), axis=1)
    zero_flag = jnp.logical_or(is_last_partial, jnp.arange(n_blk, dtype=I32) >= n_used).astype(I32)
    return dest, blk_ea.astype(I32), blk_eb.astype(I32), n_used.reshape(1), zero_flag, n_blk * blk


def _moe(x, g, mod_l, w_r_t, b_r, w1, wd, seq, tail, tail_args):
    n_e = w_r_t.shape[0]
    h_tiles, meta_i, counts = _router(x, g, mod_l, w_r_t, b_r, seq)
    dest, blk_ea, blk_eb, n_used, zero_flag, n_rows = _moe_plan(meta_i, counts, n_e)
    xs = _dispatch(zero_flag, dest, h_tiles, n_rows)
    y = _experts(blk_ea, blk_eb, n_used, xs, w1, wd)
    return _combine(dest, y, x, mod_l, seq, tail, tail_args)


def kernel(x, c, norm1_g, norm2_g, w_ada, b_ada, sb_w_qkv, sb_w_o, cv_w_in, cv_b_in, cv_w_dw, cv_b_dw, cv_ln_g, cv_ln_b, cv_w_out, cv_b_out, w_router, b_router, moe_w_gate, moe_w_up, moe_w_down, final_g):
    batch, seq, d = x.shape
    depth = w_ada.shape[0]
    assert d == N_HEADS * (LANES // 2) and seq % ROW_TILE == 0 and seq % ATT_BLOCK == 0
    assert (batch * seq) % EXPERT_BLOCK == 0 and w_router.shape[1] == 4 * N_GROUPS

    mod = _ada(c, w_ada, b_ada).reshape(depth, batch, 6, d)
    w_r_t = w_router.T.astype(BF16)
    xf = x.reshape(batch * seq, d)
    proj = _qkv_proj(xf, norm1_g[0], mod[0], sb_w_qkv[0].astype(BF16), seq)
    for layer in range(depth):
        mod_l = mod[layer]
        j = layer // 2
        if layer % 2 == 0:
            o = _sb_attention(proj, batch, seq, d)
            xf = _out_proj(o, sb_w_o[j].astype(BF16), xf, mod_l, seq)
        else:
            xf = _cv_out(proj, cv_w_dw[j], cv_b_dw[j], cv_ln_g[j], cv_ln_b[j],
                         cv_w_out[j].astype(BF16), cv_b_out[j], xf, mod_l, seq)
        w1 = jnp.concatenate([moe_w_gate[layer], moe_w_up[layer]], axis=-1).astype(BF16)
        wd = moe_w_down[layer].astype(BF16)
        nxt = layer + 1
        if nxt == depth:
            tail, tail_args = "final", (final_g,)
        elif nxt % 2 == 0:
            tail, tail_args = "qkv", (norm1_g[nxt], mod[nxt], sb_w_qkv[nxt // 2].astype(BF16))
        else:
            tail, tail_args = "cv_in", (norm1_g[nxt], mod[nxt], cv_w_in[nxt // 2].astype(BF16), cv_b_in[nxt // 2])
        out = _moe(xf, norm2_g[layer], mod_l, w_r_t, b_router, w1, wd, seq, tail, tail_args)
        if nxt == depth:
            xf = out
        else:
            xf, proj = out
    return xf.reshape(batch, seq, d)
```

```python
import functools

import jax
import jax.numpy as jnp
from jax import lax
from jax.experimental import pallas as pl
from jax.experimental.pallas import tpu as pltpu

F32 = jnp.float32
BF16 = jnp.bfloat16
I32 = jnp.int32
U32 = jnp.uint32

N_HEADS = 16
N_GROUPS = 4
NORM_EPS = 1e-6
LOG2_E = 1.4426950408889634
MASKED_LOG = -1e30

LANES = 128
SUBLANES = 8
Q_SCALE = LOG2_E / (LANES // 2) ** 0.5
MXU_DIM = 256
VMEM_LIMIT = 56 * 1024 * 1024

ROW_TILE = 512
ATT_BLOCK = MXU_DIM
ATT_PAIRS = 8
CONV_TILE = 256
CONV_HALO = 32
CONV_CHUNK = 128
EXPERT_BLOCK = 256
ROUTE_TILE = 512
MOVE_TILE = 1024
N_CLASS_PAD = 32


def _cparams(sem):
    return pltpu.CompilerParams(dimension_semantics=sem, vmem_limit_bytes=VMEM_LIMIT)


def _sigmoid(x):
    return 1.0 / (1.0 + jnp.exp(-x))


def _norm_mod(x, g, shift, scale):
    ms = jnp.mean(x * x, axis=-1, keepdims=True)
    y = x * lax.rsqrt(ms + NORM_EPS) * g
    return y * (1.0 + scale) + shift


def _ada_kernel(c_ref, w_ref, b_ref, o_ref):
    c = c_ref[...]
    cond = (c * _sigmoid(c)).astype(BF16)
    o_ref[...] = jnp.dot(cond, w_ref[...].astype(BF16), preferred_element_type=F32) + b_ref[...]


def _ada(c, w_ada, b_ada):
    depth, d, n = w_ada.shape
    b = c.shape[0]
    tn = 1536
    return pl.pallas_call(
        _ada_kernel,
        grid=(depth, n // tn),
        in_specs=[pl.BlockSpec((b, d), lambda l, j: (0, 0)),
                  pl.BlockSpec((None, d, tn), lambda l, j: (l, 0, j)),
                  pl.BlockSpec((None, 1, tn), lambda l, j: (l, 0, j))],
        out_specs=pl.BlockSpec((None, b, tn), lambda l, j: (l, 0, j)),
        out_shape=jax.ShapeDtypeStruct((depth, b, n), F32),
        compiler_params=_cparams(("parallel", "parallel")),
        name="ada",
    )(c, w_ada, b_ada.reshape(depth, 1, n))


def _store_qkv(h, w_ref, qkv_ref):
    d = h.shape[1]
    for j in range(qkv_ref.shape[1] // d):
        acc = jnp.dot(h, w_ref[:, j * d:(j + 1) * d], preferred_element_type=F32)
        if j == 0:
            acc = acc * Q_SCALE
        qkv_ref[:, j * d:(j + 1) * d] = acc.astype(qkv_ref.dtype)


def _qkv_kernel(x_ref, g_ref, mod_ref, w_ref, o_ref):
    h = _norm_mod(x_ref[...], g_ref[...], mod_ref[0:1, :], mod_ref[1:2, :]).astype(BF16)
    _store_qkv(h, w_ref, o_ref)


def _qkv_proj(x, g, mod_l, w, seq):
    t, d = x.shape
    n = w.shape[1]
    tm = ROW_TILE
    per_seq = seq // tm
    return pl.pallas_call(
        _qkv_kernel,
        grid=(t // tm,),
        in_specs=[pl.BlockSpec((tm, d), lambda i: (i, 0)),
                  pl.BlockSpec((1, d), lambda i: (0, 0)),
                  pl.BlockSpec((None, 6, d), lambda i: (i // per_seq, 0, 0)),
                  pl.BlockSpec((d, n), lambda i: (0, 0))],
        out_specs=pl.BlockSpec((tm, n), lambda i: (i, 0)),
        out_shape=jax.ShapeDtypeStruct((t, n), BF16),
        compiler_params=_cparams(("parallel",)),
        name="sb_qkv",
    )(x, g.reshape(1, d), mod_l, w)


def _route_plumbing(t, tm, d, n_e):
    row = lambda i: (0, 0)
    in_specs = [pl.BlockSpec((1, d), row), pl.BlockSpec((n_e, d), row), pl.BlockSpec((n_e, 1), row)]
    out_specs = [pl.BlockSpec((tm * SUBLANES, LANES), lambda i: (i, 0)),
                 pl.BlockSpec((SUBLANES, tm), lambda i: (0, i)),
                 pl.BlockSpec((N_CLASS_PAD, LANES), row)]
    out_shape = [jax.ShapeDtypeStruct((t * SUBLANES, LANES), U32),
                 jax.ShapeDtypeStruct((SUBLANES, t), I32),
                 jax.ShapeDtypeStruct((N_CLASS_PAD, LANES), F32)]
    return in_specs, out_specs, out_shape, [pltpu.VMEM((N_CLASS_PAD, 1), F32)]


def _oproj_kernel(o_ref, w_ref, x_ref, mod_ref, g2_ref, wr_ref, br_ref,
                  out_ref, h_ref, mi_ref, cnt_ref, base_scr):
    y = jnp.dot(o_ref[...], w_ref[...], preferred_element_type=F32)
    x = x_ref[...] + mod_ref[2:3, :] * y
    out_ref[...] = x
    _route(x, g2_ref, mod_ref, wr_ref, br_ref, h_ref, mi_ref, cnt_ref, base_scr)


def _out_proj(o, w, x, mod_l, g2, w_r_t, b_r, seq):
    t, d = x.shape
    n_e = w_r_t.shape[0]
    tm = ROUTE_TILE
    per_seq = seq // tm
    r_in, r_out, r_shape, r_scratch = _route_plumbing(t, tm, d, n_e)
    return pl.pallas_call(
        _oproj_kernel,
        grid=(t // tm,),
        in_specs=[pl.BlockSpec((tm, d), lambda i: (i, 0)),
                  pl.BlockSpec((d, d), lambda i: (0, 0)),
                  pl.BlockSpec((tm, d), lambda i: (i, 0)),
                  pl.BlockSpec((None, 6, d), lambda i: (i // per_seq, 0, 0))] + r_in,
        out_specs=[pl.BlockSpec((tm, d), lambda i: (i, 0))] + r_out,
        out_shape=[jax.ShapeDtypeStruct((t, d), F32)] + r_shape,
        scratch_shapes=r_scratch,
        compiler_params=_cparams(("arbitrary",)),
        name="sb_out",
    )(o, w, x, mod_l, g2.reshape(1, d), w_r_t, b_r.reshape(n_e, 1))


def _sb_attn_kernel(q_ref, k_ref, v_ref, o_ref, r_scr, o_scr):
    seq = q_ref.shape[0]
    tb = ATT_BLOCK
    half = LANES // 2
    head0 = lax.broadcasted_iota(I32, (1, LANES), 1) < half
    row = lax.broadcasted_iota(I32, (tb, tb), 0)
    col = lax.broadcasted_iota(I32, (tb, tb), 1)
    causal = col < row
    causal2 = jnp.concatenate([causal, causal], axis=0)
    upper = (row > col).astype(BF16)
    sign = jnp.uint32(0x80000000)
    n_q = seq // tb
    pairs = range(ATT_PAIRS)

    def lanes(p):
        return slice(p * LANES, (p + 1) * LANES)

    def q_block(qi, carry):
        q0 = pl.multiple_of(qi * tb, tb)
        qq = []
        for p in pairs:
            q2 = q_ref[pl.ds(q0, tb), lanes(p)]
            zero = jnp.zeros_like(q2)
            qq.append(jnp.concatenate([jnp.where(head0, q2, zero), jnp.where(head0, zero, q2)], axis=0))
        r_scr[...] = jnp.zeros_like(r_scr)
        o_scr[...] = jnp.zeros_like(o_scr)

        def kv_block(kb, mask):
            k0 = pl.multiple_of(kb * tb, tb)
            zs = [lax.dot_general(qq[p], k_ref[pl.ds(k0, tb), lanes(p)], (((1,), (1,)), ((), ())),
                                  preferred_element_type=F32) for p in pairs]
            log_beta, log_1m = [], []
            for p in pairs:
                z = zs[p]
                neg_abs = pltpu.bitcast(pltpu.bitcast(z, U32) | sign, F32)
                lb = jnp.minimum(z, 0.0) - jnp.log2(1.0 + jnp.exp2(neg_abs))
                l1 = lb - z
                if mask is not None:
                    l1 = jnp.where(mask, l1, 0.0)
                    lb = jnp.where(mask, lb, MASKED_LOG)
                log_beta.append(lb)
                log_1m.append(l1.astype(BF16))
            inner = [jnp.dot(log_1m[p], upper, preferred_element_type=F32) for p in pairs]
            probs = [jnp.exp2(log_beta[p] + inner[p]).astype(BF16) for p in pairs]
            for p in pairs:
                r = r_scr[p]
                pv = jnp.dot(probs[p], v_ref[pl.ds(k0, tb), lanes(p)], preferred_element_type=F32)
                o_scr[p] += jnp.exp2(r) * pv
                r_scr[p] = r + inner[p][:, 0:1] + log_1m[p][:, 0:1].astype(F32)

        kv_block(qi, causal2)

        def below(i, c):
            kv_block(qi - 1 - i, None)
            return c

        lax.fori_loop(0, qi, below, 0)
        for p in pairs:
            o_ref[pl.ds(q0, tb), lanes(p)] = jnp.where(head0, o_scr[p, :tb], o_scr[p, tb:]).astype(o_ref.dtype)
        return carry

    lax.fori_loop(0, n_q, q_block, 0)


def _sb_attention(qkv, batch, seq, d):
    t = batch * seq
    width = ATT_PAIRS * LANES
    groups = d // width
    tb = ATT_BLOCK
    return pl.pallas_call(
        _sb_attn_kernel,
        grid=(batch, groups),
        in_specs=[pl.BlockSpec((seq, width), lambda b, p: (b, p)),
                  pl.BlockSpec((seq, width), lambda b, p: (b, groups + p)),
                  pl.BlockSpec((seq, width), lambda b, p: (b, 2 * groups + p))],
        out_specs=pl.BlockSpec((seq, width), lambda b, p: (b, p)),
        out_shape=jax.ShapeDtypeStruct((t, d), BF16),
        scratch_shapes=[pltpu.VMEM((ATT_PAIRS, 2 * tb, 1), F32),
                        pltpu.VMEM((ATT_PAIRS, 2 * tb, LANES), F32)],
        compiler_params=_cparams(("parallel", "parallel")),
        name="sb_attn",
    )(qkv, qkv, qkv)


def _cv_out_kernel(u_ref, halo_ref, wdw_ref, bdw_ref, lng_ref, lnb_ref, w_ref, b_ref, x_ref, mod_ref,
                   g2_ref, wr_ref, br_ref, out_ref, h_ref, mi_ref, cnt_ref, ext_scr, conv_scr, base_scr,
                   *, per_seq):
    tm, d = u_ref.shape
    width = wdw_ref.shape[0]
    first = (pl.program_id(0) % per_seq) == 0
    halo = halo_ref[...]
    ext_scr[0:CONV_HALO, :] = jnp.where(first, jnp.zeros_like(halo), halo)
    ext_scr[CONV_HALO:, :] = u_ref[...]
    lead = CONV_HALO - (width - 1)
    rows = CONV_CHUNK
    win = rows + CONV_HALO
    for rc in range(tm // rows):
        for c in range(d // LANES):
            lanes = slice(c * LANES, (c + 1) * LANES)
            window = ext_scr[rc * rows:rc * rows + win, lanes]
            acc = jnp.zeros((rows, LANES), F32)
            for s in range(SUBLANES):
                taps = [k for k in range(width) if (lead + k) % SUBLANES == s]
                if not taps:
                    continue
                shifted = window if s == 0 else pltpu.roll(window, win - s, axis=0)
                for k in taps:
                    m = (lead + k) // SUBLANES * SUBLANES
                    acc = acc + shifted[m:m + rows, :] * wdw_ref[k:k + 1, lanes]
            conv_scr[rc * rows:(rc + 1) * rows, lanes] = acc
    v = conv_scr[...] + bdw_ref[...]
    mu = jnp.mean(v, axis=-1, keepdims=True)
    var = jnp.mean(jnp.square(v - mu), axis=-1, keepdims=True)
    y = (v - mu) * lax.rsqrt(var + NORM_EPS) * lng_ref[...] + lnb_ref[...]
    y = (y * _sigmoid(y)).astype(BF16)
    z = jnp.dot(y, w_ref[...], preferred_element_type=F32) + b_ref[...]
    x = x_ref[...] + mod_ref[2:3, :] * z
    out_ref[...] = x
    _route(x, g2_ref, mod_ref, wr_ref, br_ref, h_ref, mi_ref, cnt_ref, base_scr)


def _cv_out(u, w_dw, b_dw, ln_g, ln_b, w, b, x, mod_l, g2, w_r_t, b_r, seq):
    t, d = x.shape
    n_e = w_r_t.shape[0]
    tm = CONV_TILE
    per_seq = seq // tm
    width = w_dw.shape[0]
    assert width - 1 <= CONV_HALO and tm % CONV_HALO == 0
    halo_per_tile = tm // CONV_HALO
    row = lambda i: (0, 0)
    r_in, r_out, r_shape, r_scratch = _route_plumbing(t, tm, d, n_e)
    return pl.pallas_call(
        functools.partial(_cv_out_kernel, per_seq=per_seq),
        grid=(t // tm,),
        in_specs=[pl.BlockSpec((tm, d), lambda i: (i, 0)),
                  pl.BlockSpec((CONV_HALO, d), lambda i: (jnp.maximum(i * halo_per_tile - 1, 0), 0)),
                  pl.BlockSpec((width, d), row),
                  pl.BlockSpec((1, d), row), pl.BlockSpec((1, d), row), pl.BlockSpec((1, d), row),
                  pl.BlockSpec((d, d), row), pl.BlockSpec((1, d), row),
                  pl.BlockSpec((tm, d), lambda i: (i, 0)),
                  pl.BlockSpec((None, 6, d), lambda i: (i // per_seq, 0, 0))] + r_in,
        out_specs=[pl.BlockSpec((tm, d), lambda i: (i, 0))] + r_out,
        out_shape=[jax.ShapeDtypeStruct((t, d), F32)] + r_shape,
        scratch_shapes=[pltpu.VMEM((tm + CONV_HALO, d), F32), pltpu.VMEM((tm, d), F32)] + r_scratch,
        compiler_params=_cparams(("arbitrary",)),
        name="cv_out",
    )(u, u, w_dw, b_dw.reshape(1, d), ln_g.reshape(1, d), ln_b.reshape(1, d), w, b.reshape(1, d), x, mod_l,
      g2.reshape(1, d), w_r_t, b_r.reshape(n_e, 1))


def _top2_of4(v):
    m0, i0 = v[0], jnp.zeros(v[0].shape, I32)
    for j in range(1, 4):
        gt = v[j] > m0
        m0 = jnp.where(gt, v[j], m0)
        i0 = jnp.where(gt, j, i0)
    m1 = jnp.full(v[0].shape, -jnp.inf, F32)
    i1 = jnp.zeros(v[0].shape, I32)
    for j in range(4):
        gt = jnp.logical_and(i0 != j, v[j] > m1)
        m1 = jnp.where(gt, v[j], m1)
        i1 = jnp.where(gt, j, i1)
    return i0, i1


def _pick4(i, v):
    return jnp.where(i == 0, v[0], jnp.where(i == 1, v[1], jnp.where(i == 2, v[2], v[3])))


def _route(x, g_ref, mod_ref, wr_ref, br_ref, h_ref, mi_ref, cnt_ref, base_scr):
    tm, d = x.shape
    n_e = wr_ref.shape[0]
    epg = n_e // N_GROUPS

    @pl.when(pl.program_id(0) == 0)
    def _():
        base_scr[...] = jnp.zeros_like(base_scr)

    hb = _norm_mod(x, g_ref[...], mod_ref[3:4, :], mod_ref[4:5, :]).astype(BF16)
    bits = pltpu.bitcast(hb.astype(F32), U32)
    half = d // 2
    words = (bits[:, half:] & jnp.uint32(0xFFFF0000)) | lax.shift_right_logical(bits[:, :half], jnp.uint32(16))
    for c in range(half // LANES):
        h_ref[pl.ds(c, tm, stride=SUBLANES), :] = words[:, c * LANES:(c + 1) * LANES]
    for c in range(half // LANES + 1, SUBLANES):
        h_ref[pl.ds(c, tm, stride=SUBLANES), :] = jnp.zeros((tm, LANES), U32)
    logits = lax.dot_general(wr_ref[...], hb, (((1,), (1,)), ((), ())),
                             preferred_element_type=F32)
    scores = _sigmoid(logits)
    sel = scores + br_ref[...]
    sel_r = [sel[e:e + 1, :] for e in range(n_e)]
    sc_r = [scores[e:e + 1, :] for e in range(n_e)]

    best, gi = None, None
    for g in range(N_GROUPS):
        a, b, c, e4 = sel_r[g * epg:(g + 1) * epg]
        hi1, lo1 = jnp.maximum(a, b), jnp.minimum(a, b)
        hi2, lo2 = jnp.maximum(c, e4), jnp.minimum(c, e4)
        top1 = jnp.maximum(hi1, hi2)
        top2 = jnp.maximum(jnp.minimum(hi1, hi2), jnp.maximum(lo1, lo2))
        gs = top1 + top2
        if g == 0:
            best, gi = gs, jnp.zeros(gs.shape, I32)
        else:
            gt = gs > best
            best = jnp.where(gt, gs, best)
            gi = jnp.where(gt, g, gi)

    in_sel = [_pick4(gi, [sel_r[g * epg + j] for g in range(N_GROUPS)]) for j in range(epg)]
    in_sc = [_pick4(gi, [sc_r[g * epg + j] for g in range(N_GROUPS)]) for j in range(epg)]
    i0, i1 = _top2_of4(in_sel)
    w0, w1 = _pick4(i0, in_sc), _pick4(i1, in_sc)
    wsum = w0 + w1
    w0, w1 = w0 / wsum, w1 / wsum

    lo, hi = jnp.minimum(i0, i1), jnp.maximum(i0, i1)
    pair = jnp.where(lo == 0, 0, jnp.where(lo == 1, 3, 5)) + hi - lo - 1
    cls = gi * 6 + pair
    wa = jnp.where(i0 < i1, w0, w1)
    wb = jnp.where(i0 < i1, w1, w0)

    onehot = (lax.broadcasted_iota(I32, (N_CLASS_PAD, tm), 0) == cls).astype(BF16)
    before = (lax.broadcasted_iota(I32, (tm, tm), 0) < lax.broadcasted_iota(I32, (tm, tm), 1)).astype(BF16)
    prefix = jnp.dot(onehot, before, preferred_element_type=F32)
    base = base_scr[...]
    oh = onehot.astype(F32)
    rank = jnp.sum(oh * (prefix + base), axis=0, keepdims=True)
    base = base + jnp.sum(oh, axis=1, keepdims=True)
    base_scr[...] = base
    cnt_ref[...] = jnp.broadcast_to(base, cnt_ref.shape)

    mi_ref[...] = jnp.concatenate(
        [cls, rank.astype(I32), jnp.zeros((SUBLANES - 2, tm), I32)], axis=0)
    wrows = jnp.concatenate([wa, wb, jnp.zeros((LANES - 2, tm), F32)], axis=0)
    h_ref[pl.ds(half // LANES, tm, stride=SUBLANES), :] = pltpu.bitcast(wrows.T, U32)


def _row_copy(src, s, dst, r, sem):
    return pltpu.make_async_copy(src.at[pl.ds(pl.multiple_of(s * SUBLANES, SUBLANES), SUBLANES)],
                                 dst.at[pl.ds(pl.multiple_of(r * SUBLANES, SUBLANES), SUBLANES)], sem)


def _dispatch_kernel(zero_flag_ref, dest_ref, h_ref, xs_hbm, zero_scr, sem, zsem):
    tm = dest_ref.shape[1]
    blk = zero_scr.shape[0]
    n_blk = xs_hbm.shape[0] // blk
    i = pl.program_id(0)

    @pl.when(i == 0)
    def _():
        zero_scr[...] = jnp.zeros_like(zero_scr)

        def fill(j, n):
            flag = zero_flag_ref[j]

            @pl.when(flag == 1)
            def _():
                pltpu.make_async_copy(zero_scr, xs_hbm.at[pl.ds(pl.multiple_of(j * blk, blk), blk)], zsem).start()

            return n + flag

        n_fill = lax.fori_loop(0, n_blk, fill, 0)

        def drain(j, c):
            pltpu.make_async_copy(zero_scr, xs_hbm.at[pl.ds(0, blk)], zsem).wait()
            return c

        lax.fori_loop(0, n_fill, drain, 0)

    _issue_rows(lambda r: _row_copy(h_ref, r, xs_hbm, dest_ref[0, r], sem), tm)
    pltpu.make_async_copy(h_ref, xs_hbm.at[pl.ds(0, tm * SUBLANES)], sem).wait()


def _dispatch(zero_flag, dest, h_tiles, n_rows):
    t = h_tiles.shape[0] // SUBLANES
    tm = MOVE_TILE
    grid_spec = pltpu.PrefetchScalarGridSpec(
        num_scalar_prefetch=1,
        grid=(t // tm,),
        in_specs=[pl.BlockSpec((None, 1, tm), lambda i, zf: (i, 0, 0), memory_space=pltpu.SMEM),
                  pl.BlockSpec((tm * SUBLANES, LANES), lambda i, zf: (i, 0))],
        out_specs=pl.BlockSpec(memory_space=pl.ANY),
        scratch_shapes=[pltpu.VMEM((EXPERT_BLOCK * SUBLANES, LANES), U32),
                        pltpu.SemaphoreType.DMA(()), pltpu.SemaphoreType.DMA(())],
    )
    return pl.pallas_call(
        _dispatch_kernel,
        grid_spec=grid_spec,
        out_shape=jax.ShapeDtypeStruct((n_rows * SUBLANES, LANES), U32),
        compiler_params=_cparams(("arbitrary",)),
        name="moe_dispatch",
    )(zero_flag, dest.reshape(t // tm, 1, tm), h_tiles)


def _swiglu(x, w1_ref, wd_ref):
    f = wd_ref.shape[0]
    gu = jnp.dot(x, w1_ref[...], preferred_element_type=F32)
    gate, up = gu[:, :f], gu[:, f:]
    hidden = (gate * _sigmoid(gate) * up).astype(BF16)
    return jnp.dot(hidden, wd_ref[...], preferred_element_type=F32)


def _expert_kernel(ea_ref, eb_ref, used_ref, xs_ref, w1a_ref, wda_ref, w1b_ref, wdb_ref, y_ref):
    blk = xs_ref.shape[0] // SUBLANES
    d = w1a_ref.shape[0]
    n_word = d // 2 // LANES
    j = pl.program_id(0)

    @pl.when(j < used_ref[0])
    def _():
        words = [xs_ref[pl.ds(c, blk, stride=SUBLANES), :] for c in range(n_word)]
        lo = [pltpu.bitcast(lax.shift_left(w, jnp.uint32(16)), F32).astype(BF16) for w in words]
        hi = [pltpu.bitcast(w & jnp.uint32(0xFFFF0000), F32).astype(BF16) for w in words]
        x = jnp.concatenate(lo + hi, axis=-1)
        meta = pltpu.bitcast(xs_ref[pl.ds(n_word, blk, stride=SUBLANES), :], F32)
        wa, wb = meta[:, 0:1], meta[:, 1:2]
        y = wa * _swiglu(x, w1a_ref, wda_ref) + wb * _swiglu(x, w1b_ref, wdb_ref)
        for c in range(d // LANES):
            y_ref[pl.ds(c, blk, stride=SUBLANES), :] = y[:, c * LANES:(c + 1) * LANES]

    @pl.when(j >= used_ref[0])
    def _():
        y_ref[...] = jnp.zeros_like(y_ref)


def _experts(blk_ea, blk_eb, n_used, xs, w1, wd):
    n_e, d, f2 = w1.shape
    f = f2 // 2
    blk = EXPERT_BLOCK
    n_blk = xs.shape[0] // (blk * SUBLANES)
    assert d // LANES == SUBLANES
    grid_spec = pltpu.PrefetchScalarGridSpec(
        num_scalar_prefetch=3,
        grid=(n_blk,),
        in_specs=[pl.BlockSpec((blk * SUBLANES, LANES), lambda j, ea, eb, nu: (j, 0)),
                  pl.BlockSpec((None, d, f2), lambda j, ea, eb, nu: (ea[j], 0, 0)),
                  pl.BlockSpec((None, f, d), lambda j, ea, eb, nu: (ea[j], 0, 0)),
                  pl.BlockSpec((None, d, f2), lambda j, ea, eb, nu: (eb[j], 0, 0)),
                  pl.BlockSpec((None, f, d), lambda j, ea, eb, nu: (eb[j], 0, 0))],
        out_specs=pl.BlockSpec((blk * SUBLANES, LANES), lambda j, ea, eb, nu: (j, 0)),
    )
    return pl.pallas_call(
        _expert_kernel,
        grid_spec=grid_spec,
        out_shape=jax.ShapeDtypeStruct(xs.shape, F32),
        compiler_params=_cparams(("arbitrary",)),
        name="moe_experts",
    )(blk_ea, blk_eb, n_used, xs, w1, wd, w1, wd)


def _issue_rows(copy_of_row, n_rows):
    def group(g, c):
        for k in range(SUBLANES):
            copy_of_row(g * SUBLANES + k).start(priority=k % 2)
        return c

    lax.fori_loop(0, n_rows // SUBLANES, group, 0)


def _moe_update(dest_ref, dest_next_ref, y_hbm, x_ref, mod_ref, y_scr, sem):
    tm, d = x_ref.shape
    i = pl.program_id(0)

    def request(idx_ref, slot):
        _issue_rows(lambda r: _row_copy(y_hbm, idx_ref[0, r], y_scr.at[slot], r, sem.at[slot]), tm)

    @pl.when(i == 0)
    def _():
        request(dest_ref, 0)

    @pl.when(i + 1 < pl.num_programs(0))
    def _():
        request(dest_next_ref, (i + 1) % 2)

    slot = i % 2
    pltpu.make_async_copy(y_hbm.at[pl.ds(0, tm * SUBLANES)], y_scr.at[slot], sem.at[slot]).wait()
    cols = [x_ref[:, c * LANES:(c + 1) * LANES]
            + mod_ref[5:6, c * LANES:(c + 1) * LANES] * y_scr[slot, pl.ds(c, tm, stride=SUBLANES), :]
            for c in range(d // LANES)]
    return jnp.concatenate(cols, axis=-1)


def _combine_final_kernel(dest_ref, dest_next_ref, y_hbm, x_ref, mod_ref, fg_ref, out_ref, y_scr, sem):
    x = _moe_update(dest_ref, dest_next_ref, y_hbm, x_ref, mod_ref, y_scr, sem)
    ms = jnp.mean(x * x, axis=-1, keepdims=True)
    out_ref[...] = x * lax.rsqrt(ms + NORM_EPS) * fg_ref[...]


def _combine_qkv_kernel(dest_ref, dest_next_ref, y_hbm, x_ref, mod_ref, g_ref, modn_ref, w_ref,
                        x_out_ref, qkv_ref, y_scr, sem):
    x = _moe_update(dest_ref, dest_next_ref, y_hbm, x_ref, mod_ref, y_scr, sem)
    x_out_ref[...] = x
    h = _norm_mod(x, g_ref[...], modn_ref[0:1, :], modn_ref[1:2, :]).astype(BF16)
    _store_qkv(h, w_ref, qkv_ref)


def _combine_cvin_kernel(dest_ref, dest_next_ref, y_hbm, x_ref, mod_ref, g_ref, modn_ref, w_ref, b_ref,
                         x_out_ref, u_ref, y_scr, sem):
    d = x_ref.shape[1]
    x = _moe_update(dest_ref, dest_next_ref, y_hbm, x_ref, mod_ref, y_scr, sem)
    x_out_ref[...] = x
    h = _norm_mod(x, g_ref[...], modn_ref[0:1, :], modn_ref[1:2, :]).astype(BF16)
    a = jnp.dot(h, w_ref[:, :d], preferred_element_type=F32) + b_ref[:, :d]
    gate = jnp.dot(h, w_ref[:, d:], preferred_element_type=F32) + b_ref[:, d:]
    u_ref[...] = a * _sigmoid(gate)


def _combine(dest, y, x, mod_l, seq, tail, tail_args):
    t, d = x.shape
    tm = ROW_TILE
    per_seq = seq // tm
    n_tiles = t // tm
    row = lambda i: (0, 0)
    tile = pl.BlockSpec((tm, d), lambda i: (i, 0))
    mod_spec = pl.BlockSpec((None, 6, d), lambda i: (i // per_seq, 0, 0))
    in_specs = [pl.BlockSpec((None, 1, tm), lambda i: (i, 0, 0), memory_space=pltpu.SMEM),
                pl.BlockSpec((None, 1, tm), lambda i: (jnp.minimum(i + 1, n_tiles - 1), 0, 0),
                             memory_space=pltpu.SMEM),
                pl.BlockSpec(memory_space=pl.ANY), tile, mod_spec]
    dest3 = dest.reshape(n_tiles, 1, tm)
    args = [dest3, dest3, y, x, mod_l]
    x_shape = jax.ShapeDtypeStruct((t, d), F32)
    if tail == "final":
        (final_g,) = tail_args
        body, out_specs, out_shape = _combine_final_kernel, tile, x_shape
        in_specs += [pl.BlockSpec((1, d), row)]
        args += [final_g.reshape(1, d)]
    elif tail == "qkv":
        g, mod_n, w = tail_args
        n = w.shape[1]
        body = _combine_qkv_kernel
        out_specs = [tile, pl.BlockSpec((tm, n), lambda i: (i, 0))]
        out_shape = [x_shape, jax.ShapeDtypeStruct((t, n), BF16)]
        in_specs += [pl.BlockSpec((1, d), row), mod_spec, pl.BlockSpec((d, n), row)]
        args += [g.reshape(1, d), mod_n, w]
    else:
        g, mod_n, w, b = tail_args
        body = _combine_cvin_kernel
        out_specs = [tile, tile]
        out_shape = [x_shape, x_shape]
        in_specs += [pl.BlockSpec((1, d), row), mod_spec, pl.BlockSpec((d, 2 * d), row),
                     pl.BlockSpec((1, 2 * d), row)]
        args += [g.reshape(1, d), mod_n, w, b.reshape(1, 2 * d)]
    return pl.pallas_call(
        body,
        grid=(n_tiles,),
        in_specs=in_specs,
        out_specs=out_specs,
        out_shape=out_shape,
        scratch_shapes=[pltpu.VMEM((2, tm * SUBLANES, LANES), F32), pltpu.SemaphoreType.DMA((2,))],
        compiler_params=_cparams(("arbitrary",)),
        name="moe_combine_" + tail,
    )(*args)


def _moe_plan(meta_i, counts, n_e):
    n_cls = N_GROUPS * 6
    epg = n_e // N_GROUPS
    blk = EXPERT_BLOCK
    t = meta_i.shape[1]
    n_blk = t // blk + n_cls
    cls, rank = meta_i[0], meta_i[1]
    cnt = counts[:n_cls, 0].astype(I32)
    padded = (cnt + blk - 1) // blk * blk
    pad_end = jnp.cumsum(padded)
    pad_start = pad_end - padded
    dest = jnp.sum(jnp.where(cls[None, :] == jnp.arange(n_cls, dtype=I32)[:, None], pad_start[:, None], 0),
                   axis=0).astype(I32) + rank
    n_used = (pad_end[-1] // blk).astype(I32)
    blk_start = jnp.arange(n_blk, dtype=I32) * blk
    blk_cls = jnp.minimum(jnp.sum(blk_start[:, None] >= pad_end[None, :], axis=1), n_cls - 1).astype(I32)
    pair_lo = jnp.array([0, 0, 0, 1, 1, 2], I32)
    pair_hi = jnp.array([1, 2, 3, 2, 3, 3], I32)
    blk_ea = (blk_cls // 6) * epg + pair_lo[blk_cls % 6]
    blk_eb = (blk_cls // 6) * epg + pair_hi[blk_cls % 6]
    last_blk = pad_end // blk - 1
    partial = jnp.logical_and(cnt % blk != 0, cnt > 0)
    is_last_partial = jnp.any(jnp.logical_and(jnp.arange(n_blk, dtype=I32)[:, None] == last_blk[None, :],
                                              partial[None, :]), axis=1)
    zero_flag = jnp.logical_or(is_last_partial, jnp.arange(n_blk, dtype=I32) >= n_used).astype(I32)
    return dest, blk_ea.astype(I32), blk_eb.astype(I32), n_used.reshape(1), zero_flag, n_blk * blk


def _moe(x, routed, mod_l, n_e, w1, wd, seq, tail, tail_args):
    h_tiles, meta_i, counts = routed
    dest, blk_ea, blk_eb, n_used, zero_flag, n_rows = _moe_plan(meta_i, counts, n_e)
    xs = _dispatch(zero_flag, dest, h_tiles, n_rows)
    y = _experts(blk_ea, blk_eb, n_used, xs, w1, wd)
    return _combine(dest, y, x, mod_l, seq, tail, tail_args)


def kernel(x, c, norm1_g, norm2_g, w_ada, b_ada, sb_w_qkv, sb_w_o, cv_w_in, cv_b_in, cv_w_dw, cv_b_dw, cv_ln_g, cv_ln_b, cv_w_out, cv_b_out, w_router, b_router, moe_w_gate, moe_w_up, moe_w_down, final_g):
    batch, seq, d = x.shape
    depth = w_ada.shape[0]
    assert d == N_HEADS * (LANES // 2) and seq % ROW_TILE == 0 and seq % ATT_BLOCK == 0
    assert (batch * seq) % EXPERT_BLOCK == 0 and w_router.shape[1] == 4 * N_GROUPS

    mod = _ada(c, w_ada, b_ada).reshape(depth, batch, 6, d)
    w_r_t = w_router.T.astype(BF16)
    xf = x.reshape(batch * seq, d)
    proj = _qkv_proj(xf, norm1_g[0], mod[0], sb_w_qkv[0].astype(BF16), seq)
    for layer in range(depth):
        mod_l = mod[layer]
        j = layer // 2
        route_args = (norm2_g[layer], w_r_t, b_router, seq)
        if layer % 2 == 0:
            o = _sb_attention(proj, batch, seq, d)
            xf, *routed = _out_proj(o, sb_w_o[j].astype(BF16), xf, mod_l, *route_args)
        else:
            xf, *routed = _cv_out(proj, cv_w_dw[j], cv_b_dw[j], cv_ln_g[j], cv_ln_b[j],
                                  cv_w_out[j].astype(BF16), cv_b_out[j], xf, mod_l, *route_args)
        w1 = jnp.concatenate([moe_w_gate[layer], moe_w_up[layer]], axis=-1).astype(BF16)
        wd = moe_w_down[layer].astype(BF16)
        nxt = layer + 1
        if nxt == depth:
            tail, tail_args = "final", (final_g,)
        elif nxt % 2 == 0:
            tail, tail_args = "qkv", (norm1_g[nxt], mod[nxt], sb_w_qkv[nxt // 2].astype(BF16))
        else:
            tail, tail_args = "cv_in", (norm1_g[nxt], mod[nxt], cv_w_in[nxt // 2].astype(BF16), cv_b_in[nxt // 2])
        out = _moe(xf, routed, mod_l, w_router.shape[1], w1, wd, seq, tail, tail_args)
        if nxt == depth:
            xf = out
        else:
            xf, proj = out
    return xf.reshape(batch, seq, d)
```
